```python
import math
import jax, jax.numpy as jnp
from jax import lax
import numpy as np

D_MODEL = 2048
BATCH = 2
SEQ = 8192
DEPTH = 1

HEAD_DIM = 128
MIX_WIDTH = D_MODEL
FOX_HEADS = MIX_WIDTH // 2 // HEAD_DIM
DIFF_HEADS = MIX_WIDTH // 2 // (2 * HEAD_DIM)
FOX_WIDTH = FOX_HEADS * HEAD_DIM
DIFF_WIDTH = DIFF_HEADS * 2 * HEAD_DIM
IN_SPLIT_SIZES = (FOX_WIDTH, FOX_WIDTH, FOX_WIDTH, FOX_HEADS, DIFF_WIDTH, DIFF_WIDTH, DIFF_WIDTH)
IN_COLS = sum(IN_SPLIT_SIZES)
Q_BLOCK = 128
N_GROUPS = 8
EXPERTS_PER_GROUP = 8
N_EXPERTS = N_GROUPS * EXPERTS_PER_GROUP
TOP_K_IN_GROUP = 2
D_FF = D_MODEL // 2
MOE_BLOCK = 128
NORM_EPS = 1e-6
SUBLN_EPS = 1e-5
FORGET_BIAS_CENTER = 3.0

kernel_name = "hybrid_fox_diffattn_hiermoe"


def rms_norm(x, g, eps=NORM_EPS):
    xf = x.astype(jnp.float32)
    y = xf * lax.rsqrt(jnp.mean(xf * xf, axis=-1, keepdims=True) + eps)
    return (y * g.astype(jnp.float32)).astype(x.dtype)


def alibi_slopes(n_heads):
    return jnp.asarray(2.0 ** (-8.0 * np.arange(1, n_heads + 1) / n_heads), dtype=jnp.float32)


def forgetting_attention(q, k, v, log_f):
    S = q.shape[2]
    scale = HEAD_DIM ** -0.5
    c = jnp.cumsum(log_f, axis=-1)
    outs = []
    for blk in range(S // Q_BLOCK):
        q0, q1 = blk * Q_BLOCK, (blk + 1) * Q_BLOCK
        logits = jnp.einsum('bhqd,bhkd->bhqk', q[:, :, q0:q1], k[:, :, :q1],
                            preferred_element_type=jnp.float32) * scale
        logits = logits + c[:, :, q0:q1, None] - c[:, :, None, :q1]
        mask = (q0 + jnp.arange(Q_BLOCK))[:, None] >= jnp.arange(q1)[None, :]
        p = jax.nn.softmax(jnp.where(mask, logits, -jnp.inf), axis=-1)
        outs.append(jnp.einsum('bhqk,bhkd->bhqd', p.astype(v.dtype), v[:, :, :q1]))
    return jnp.concatenate(outs, axis=2)


def differential_attention(q, k, v, lam, slopes):
    S = q.shape[3]
    scale = HEAD_DIM ** -0.5
    outs = []
    for blk in range(S // Q_BLOCK):
        q0, q1 = blk * Q_BLOCK, (blk + 1) * Q_BLOCK
        logits = jnp.einsum('bhiqd,bhikd->bhiqk', q[:, :, :, q0:q1], k[:, :, :, :q1],
                            preferred_element_type=jnp.float32) * scale
        dist = ((q0 + jnp.arange(Q_BLOCK))[:, None] - jnp.arange(q1)[None, :]).astype(jnp.float32)
        logits = logits - slopes[None, :, None, None, None] * dist
        p = jax.nn.softmax(jnp.where(dist >= 0, logits, -jnp.inf), axis=-1)
        w = p[:, :, 0] - lam * p[:, :, 1]
        outs.append(jnp.einsum('bhqk,bhkd->bhqd', w.astype(v.dtype), v[:, :, :q1]))
    return jnp.concatenate(outs, axis=2)


def hierarchical_moe(x, wg, bg, we, be, w_gate, w_up, w_down):
    B, S, D = x.shape
    N = B * S
    xf = x.reshape(N, D)
    group_probs = jax.nn.softmax((xf @ wg + bg).astype(jnp.float32), axis=-1)
    g_prob, g_idx = lax.top_k(group_probs, 1)
    exp_logits = (xf @ we + be).astype(jnp.float32).reshape(N, N_GROUPS, EXPERTS_PER_GROUP)
    idx = jnp.broadcast_to(g_idx[:, :, None], (N, 1, EXPERTS_PER_GROUP))
    in_group = jnp.take_along_axis(exp_logits, idx, axis=1)[:, 0]
    e_prob, e_local = lax.top_k(jax.nn.softmax(in_group, axis=-1), TOP_K_IN_GROUP)
    gates = g_prob * e_prob / jnp.sum(e_prob, axis=-1, keepdims=True)
    expert_id = g_idx * EXPERTS_PER_GROUP + e_local

    A = N * TOP_K_IN_GROUP
    e_flat = expert_id.reshape(A)
    tok_flat = jnp.repeat(jnp.arange(N, dtype=jnp.int32), TOP_K_IN_GROUP)
    gate_flat = gates.reshape(A)
    order = jnp.argsort(e_flat)
    e_sorted = e_flat[order]
    counts = jnp.bincount(e_flat, length=N_EXPERTS)
    starts = jnp.cumsum(counts) - counts
    padded = ((counts + MOE_BLOCK - 1) // MOE_BLOCK) * MOE_BLOCK
    pends = jnp.cumsum(padded)
    pstarts = pends - padded
    dest = pstarts[e_sorted] + (jnp.arange(A) - starts[e_sorted])
    n_rows = ((A + N_EXPERTS * (MOE_BLOCK - 1) + MOE_BLOCK - 1) // MOE_BLOCK) * MOE_BLOCK
    n_blocks = n_rows // MOE_BLOCK
    row_tok = jnp.full((n_rows,), N, jnp.int32).at[dest].set(tok_flat[order])
    row_gate = jnp.zeros((n_rows,), jnp.float32).at[dest].set(gate_flat[order])
    block_expert = jnp.minimum(
        jnp.searchsorted(pends, jnp.arange(n_blocks) * MOE_BLOCK, side='right'), N_EXPERTS - 1)
    x_pad = jnp.concatenate([xf, jnp.zeros((1, D), xf.dtype)], axis=0)
    x_rows = x_pad[row_tok].reshape(n_blocks, MOE_BLOCK, D)

    def expert_block(args):
        xb, e = args
        hdn = jax.nn.silu(xb @ w_gate[e]) * (xb @ w_up[e])
        return hdn @ w_down[e]

    y_rows = lax.map(expert_block, (x_rows, block_expert)).reshape(n_rows, D)
    y = jnp.zeros((N + 1, D), jnp.float32).at[row_tok].add(
        y_rows.astype(jnp.float32) * row_gate[:, None])
    return y[:N].astype(x.dtype).reshape(B, S, D)


def setup_inputs(seed: int = 0) -> dict:
    key = jax.random.key(seed)
    ks = jax.random.split(key, 20)
    L, D = DEPTH, D_MODEL
    nrm = lambda k, shape, s: jax.random.normal(k, shape, jnp.float32) * s
    return {
        "x": nrm(ks[0], (BATCH, SEQ, D), 1.0),
        "attn_norm_g": 1.0 + nrm(ks[1], (L, D), 0.02),
        "w_in": nrm(ks[2], (L, D, IN_COLS), D ** -0.5),
        "forget_bias": FORGET_BIAS_CENTER + nrm(ks[3], (L, FOX_HEADS), 0.5),
        "lambda_q1": nrm(ks[4], (L, HEAD_DIM), 0.1),
        "lambda_k1": nrm(ks[5], (L, HEAD_DIM), 0.1),
        "lambda_q2": nrm(ks[6], (L, HEAD_DIM), 0.1),
        "lambda_k2": nrm(ks[7], (L, HEAD_DIM), 0.1),
        "diff_subln_g": 1.0 + nrm(ks[8], (L, 2 * HEAD_DIM), 0.02),
        "w_out": nrm(ks[9], (L, MIX_WIDTH, D), MIX_WIDTH ** -0.5),
        "ffn_norm_g": 1.0 + nrm(ks[10], (L, D), 0.02),
        "router_group_w": nrm(ks[11], (L, D, N_GROUPS), D ** -0.5),
        "router_group_b": nrm(ks[12], (L, N_GROUPS), 0.01),
        "router_expert_w": nrm(ks[13], (L, D, N_EXPERTS), D ** -0.5),
        "router_expert_b": nrm(ks[14], (L, N_EXPERTS), 0.01),
        "w_gate": nrm(ks[15], (L, N_EXPERTS, D, D_FF), D ** -0.5),
        "w_up": nrm(ks[16], (L, N_EXPERTS, D, D_FF), D ** -0.5),
        "w_down": nrm(ks[17], (L, N_EXPERTS, D_FF, D), D_FF ** -0.5),
        "final_norm_g": 1.0 + nrm(ks[18], (D,), 0.02),
    }


def reference(x, attn_norm_g, w_in, forget_bias, lambda_q1, lambda_k1, lambda_q2, lambda_k2,
              diff_subln_g, w_out, ffn_norm_g, router_group_w, router_group_b,
              router_expert_w, router_expert_b, w_gate, w_up, w_down, final_norm_g):
    B, S, _ = x.shape
    slopes = alibi_slopes(DIFF_HEADS)
    split_points = [int(v) for v in np.cumsum(IN_SPLIT_SIZES)[:-1]]
    h = x
    for l in range(DEPTH):
        a = rms_norm(h, attn_norm_g[l])
        proj = a @ w_in[l]
        fq, fk, fv, ff, dq, dk, dv = jnp.split(proj, split_points, axis=-1)
        to_heads = lambda t: t.reshape(B, S, FOX_HEADS, HEAD_DIM).transpose(0, 2, 1, 3)
        log_f = jax.nn.log_sigmoid((ff + forget_bias[l]).astype(jnp.float32)).transpose(0, 2, 1)
        fox = forgetting_attention(to_heads(fq), to_heads(fk), to_heads(fv), log_f)
        fox = fox.transpose(0, 2, 1, 3).reshape(B, S, FOX_WIDTH)

        to_pair = lambda t: t.reshape(B, S, DIFF_HEADS, 2, HEAD_DIM).transpose(0, 2, 3, 1, 4)
        dvh = dv.reshape(B, S, DIFF_HEADS, 2 * HEAD_DIM).transpose(0, 2, 1, 3)
        lam_init = 0.8 - 0.6 * math.exp(-0.3 * l)
        f32 = lambda t: t.astype(jnp.float32)
        lam = (jnp.exp(jnp.sum(f32(lambda_q1[l]) * f32(lambda_k1[l])))
               - jnp.exp(jnp.sum(f32(lambda_q2[l]) * f32(lambda_k2[l]))) + lam_init)
        diff = differential_attention(to_pair(dq), to_pair(dk), dvh, lam, slopes)
        diff = rms_norm(diff, diff_subln_g[l], SUBLN_EPS) * (1.0 - lam_init)
        diff = diff.transpose(0, 2, 1, 3).reshape(B, S, DIFF_WIDTH)

        h = h + jnp.concatenate([fox, diff], axis=-1) @ w_out[l]
        h = h + hierarchical_moe(rms_norm(h, ffn_norm_g[l]), router_group_w[l], router_group_b[l],
                                 router_expert_w[l], router_expert_b[l],
                                 w_gate[l], w_up[l], w_down[l])
    return rms_norm(h, final_norm_g)
```

```python
import functools
import math

import jax
import jax.numpy as jnp
import numpy as np
from jax import lax
from jax.experimental import pallas as pl
from jax.experimental.pallas import tpu as pltpu

HEAD_DIM = 128
LANES = 128
SUBLANES = 8
NORM_EPS = 1e-6
SUBLN_EPS = 1e-5
TOP_K = 2
ROW_PAD = 128
MAX_CHUNKS = 8
NEG_BIG = -1e30
VMEM_LIMIT_BYTES = 56 * 1024 * 1024

F32 = jnp.float32
BF16 = jnp.bfloat16
_dot = functools.partial(jnp.dot, preferred_element_type=F32)


def _params(*sem):
    return pltpu.CompilerParams(dimension_semantics=sem, vmem_limit_bytes=VMEM_LIMIT_BYTES)


def _split_bf16(a):
    hi = a.astype(BF16)
    lo = (a - hi.astype(F32)).astype(BF16)
    return hi, lo


def _dot_3pass(a, w):
    a_hi, a_lo = _split_bf16(a)
    w_hi, w_lo = _split_bf16(w)
    return _dot(a_hi, w_hi) + _dot(a_lo, w_hi) + _dot(a_hi, w_lo)


def _rms(x, g, eps):
    return x * lax.rsqrt(jnp.mean(x * x, axis=-1, keepdims=True) + eps) * g


def _inproj_kernel(x_ref, g_ref, w_ref, wf_ref, o_ref, ff_ref, a_scr):
    @pl.when(pl.program_id(1) == 0)
    def _():
        a = _rms(x_ref[...], g_ref[...], NORM_EPS)
        a_scr[...] = a.astype(BF16)
        ff_ref[...] = _dot_3pass(a, wf_ref[...])

    acc = _dot(a_scr[...], w_ref[...])
    for c in range(o_ref.shape[0]):
        o_ref[c] = acc[:, c * LANES:(c + 1) * LANES].astype(o_ref.dtype)


def _inproj(xf, g, w_main, w_forget, tm, tn):
    n, d = xf.shape
    cols = w_main.shape[1]
    assert tn % LANES == 0 and cols % tn == 0
    return pl.pallas_call(
        _inproj_kernel,
        grid=(n // tm, cols // tn),
        in_specs=[pl.BlockSpec((tm, d), lambda i, j: (i, 0)),
                  pl.BlockSpec((1, d), lambda i, j: (0, 0)),
                  pl.BlockSpec((d, tn), lambda i, j: (0, j)),
                  pl.BlockSpec((d, LANES), lambda i, j: (0, 0))],
        out_specs=[pl.BlockSpec((tn // LANES, tm, LANES), lambda i, j: (j, i, 0)),
                   pl.BlockSpec((tm, LANES), lambda i, j: (i, 0))],
        out_shape=[jax.ShapeDtypeStruct((cols // LANES, n, LANES), BF16),
                   jax.ShapeDtypeStruct((n, LANES), F32)],
        scratch_shapes=[pltpu.VMEM((tm, d), BF16)],
        compiler_params=_params("parallel", "arbitrary"),
        name="inproj",
    )(xf, g, w_main, w_forget)


def _forget_kernel(ff_ref, fb_ref, ccol_ref, crow_ref, carry, *, tiles_per_seq):
    @pl.when(pl.program_id(0) % tiles_per_seq == 0)
    def _():
        carry[...] = jnp.zeros_like(carry)

    lf = jax.nn.log_sigmoid(ff_ref[...] + fb_ref[...])
    tm = lf.shape[0]
    r = lax.broadcasted_iota(jnp.int32, (tm, tm), 0)
    c = lax.broadcasted_iota(jnp.int32, (tm, tm), 1)
    tri = (r >= c).astype(BF16)
    h1 = lf.astype(BF16)
    r1 = lf - h1.astype(F32)
    h2 = r1.astype(BF16)
    h3 = (r1 - h2.astype(F32)).astype(BF16)
    cs = (_dot(tri, h1) + _dot(tri, h2)) + _dot(tri, h3) + carry[...]
    ccol_ref[...] = cs
    carry[...] = cs[tm - 1:tm, :]
    crow_ref[...] = cs.T[:crow_ref.shape[0], :]


def _forget_cumsum(ff, fb, seq, tm, head_rows):
    n = ff.shape[0]
    return pl.pallas_call(
        functools.partial(_forget_kernel, tiles_per_seq=seq // tm),
        grid=(n // tm,),
        in_specs=[pl.BlockSpec((tm, LANES), lambda i: (i, 0)),
                  pl.BlockSpec((1, LANES), lambda i: (0, 0))],
        out_specs=[pl.BlockSpec((tm, LANES), lambda i: (i, 0)),
                   pl.BlockSpec((head_rows, tm), lambda i: (0, i))],
        out_shape=[jax.ShapeDtypeStruct((n, LANES), F32),
                   jax.ShapeDtypeStruct((head_rows, n), F32)],
        scratch_shapes=[pltpu.VMEM((1, LANES), F32)],
        compiler_params=_params("arbitrary"),
        name="forget",
    )(ff, fb)


def _softmax_step(s, v, m_scr, l_scr, acc_scr, mp):
    m_prev = m_scr[mp][:, :1]
    m_new = jnp.maximum(m_prev, jnp.max(s, axis=1, keepdims=True))
    alpha = jnp.exp(m_prev - m_new)
    p = jnp.exp(s - m_new)
    l_new = alpha * l_scr[mp][:, :1] + jnp.sum(p, axis=1, keepdims=True)
    acc_scr[mp] = alpha * acc_scr[mp] + _dot(p.astype(v.dtype), v)
    m_scr[mp] = jnp.broadcast_to(m_new, m_scr.shape[1:])
    l_scr[mp] = jnp.broadcast_to(l_new, l_scr.shape[1:])


def _causal_steps(i, j, tq, tk, step, finalize):
    last = ((i + 1) * tq - 1) // tk
    crosses = (j + 1) * tk - 1 > i * tq

    @pl.when(jnp.logical_and(j <= last, crosses))
    def _():
        step(True)

    @pl.when(jnp.logical_and(j <= last, jnp.logical_not(crosses)))
    def _():
        step(False)

    @pl.when(j == last)
    def _():
        finalize()


def _qk(q, k):
    return lax.dot_general(q, k, (((1,), (1,)), ((), ())), preferred_element_type=F32)


def _fox_kernel(q_ref, k_ref, v_ref, ccol_ref, crow_ref, o_ref, m_scr, l_scr, acc_scr, uq_scr,
                *, tq, tk, n_heads, scale):
    h = pl.program_id(0) % n_heads
    i, j = pl.program_id(1), pl.program_id(2)

    @pl.when(j == 0)
    def _():
        m_scr[...] = jnp.full_like(m_scr, NEG_BIG)
        l_scr[...] = jnp.zeros_like(l_scr)
        acc_scr[...] = jnp.zeros_like(acc_scr)
        lane = lax.broadcasted_iota(jnp.int32, (tq, LANES), 1)
        uq = jnp.sum(jnp.where(lane == h, ccol_ref[...], 0.0), axis=1, keepdims=True)
        uq_scr[...] = jnp.broadcast_to(uq, uq_scr.shape)

    def step(masked):
        s = _qk(q_ref[...], k_ref[...]) * scale + (uq_scr[:, :1] - crow_ref[...])
        if masked:
            rows = i * tq + lax.broadcasted_iota(jnp.int32, (tq, tk), 0)
            cols = j * tk + lax.broadcasted_iota(jnp.int32, (tq, tk), 1)
            s = jnp.where(rows >= cols, s, NEG_BIG)
        _softmax_step(s, v_ref[...], m_scr, l_scr, acc_scr, 0)

    def finalize():
        o_ref[...] = (acc_scr[0] / l_scr[0][:, :1]).astype(o_ref.dtype)

    _causal_steps(i, j, tq, tk, step, finalize)


def _fox_attention(proj, ccol, crow, batch, seq, n_heads, tq, tk):
    n = batch * seq
    nq, nk = seq // tq, seq // tk

    def kv_blk(i, j):
        return jnp.minimum(j, ((i + 1) * tq - 1) // tk)

    return pl.pallas_call(
        functools.partial(_fox_kernel, tq=tq, tk=tk, n_heads=n_heads, scale=HEAD_DIM ** -0.5),
        grid=(batch * n_heads, nq, nk),
        in_specs=[
            pl.BlockSpec((None, tq, LANES), lambda b, i, j: (b % n_heads, (b // n_heads) * nq + i, 0)),
            pl.BlockSpec((None, tk, LANES),
                         lambda b, i, j: (n_heads + b % n_heads, (b // n_heads) * nk + kv_blk(i, j), 0)),
            pl.BlockSpec((None, tk, LANES),
                         lambda b, i, j: (2 * n_heads + b % n_heads, (b // n_heads) * nk + kv_blk(i, j), 0)),
            pl.BlockSpec((tq, LANES), lambda b, i, j: ((b // n_heads) * nq + i, 0)),
            pl.BlockSpec((None, 1, tk), lambda b, i, j: (b % n_heads, 0, (b // n_heads) * nk + kv_blk(i, j))),
        ],
        out_specs=pl.BlockSpec((tq, LANES), lambda b, i, j: ((b // n_heads) * nq + i, b % n_heads)),
        out_shape=jax.ShapeDtypeStruct((n, n_heads * HEAD_DIM), BF16),
        scratch_shapes=[pltpu.VMEM((1, tq, LANES), F32), pltpu.VMEM((1, tq, LANES), F32),
                        pltpu.VMEM((1, tq, HEAD_DIM), F32), pltpu.VMEM((tq, LANES), F32)],
        compiler_params=_params("parallel", "parallel", "arbitrary"),
        name="fox",
    )(proj, proj, proj, ccol, crow)


def _diff_kernel(slope_ref, q_ref, k_ref, v_ref, lq1_ref, lk1_ref, lq2_ref, lk2_ref, g_ref, o_ref,
                 m_scr, l_scr, acc_scr, *, tq, tk, n_heads, scale, lam_init):
    slope = slope_ref[pl.program_id(0) % n_heads]
    i, j = pl.program_id(1), pl.program_id(2)

    @pl.when(j == 0)
    def _():
        m_scr[...] = jnp.full_like(m_scr, NEG_BIG)
        l_scr[...] = jnp.zeros_like(l_scr)
        acc_scr[...] = jnp.zeros_like(acc_scr)

    def step(masked):
        rows = i * tq + lax.broadcasted_iota(jnp.int32, (tq, 1), 0)
        cols = j * tk + lax.broadcasted_iota(jnp.int32, (1, tk), 1)
        bias = rows.astype(F32) * (-slope) - cols.astype(F32) * (-slope)
        v = jnp.concatenate([v_ref[0], v_ref[1]], axis=1)
        for mp in range(2):
            s = _qk(q_ref[mp], k_ref[mp]) * scale + bias
            if masked:
                s = jnp.where(rows >= cols, s, NEG_BIG)
            _softmax_step(s, v, m_scr, l_scr, acc_scr, mp)

    def finalize():
        lam = (jnp.exp(jnp.sum(lq1_ref[...] * lk1_ref[...], axis=1, keepdims=True))
               - jnp.exp(jnp.sum(lq2_ref[...] * lk2_ref[...], axis=1, keepdims=True)) + lam_init)
        d = acc_scr[0] / l_scr[0][:, :1] - lam * (acc_scr[1] / l_scr[1][:, :1])
        o_ref[...] = (_rms(d, g_ref[...], SUBLN_EPS) * (1.0 - lam_init)).astype(o_ref.dtype)

    _causal_steps(i, j, tq, tk, step, finalize)


def _diff_attention(proj, slopes, lq1, lk1, lq2, lk2, g, batch, seq, n_heads, pair0, lam_init, tq, tk):
    n = batch * seq
    nq, nk = seq // tq, seq // tk
    dv = 2 * HEAD_DIM

    def kv_blk(i, j):
        return jnp.minimum(j, ((i + 1) * tq - 1) // tk)

    vec = pl.BlockSpec((1, HEAD_DIM), lambda b, i, j, s: (0, 0))
    grid_spec = pltpu.PrefetchScalarGridSpec(
        num_scalar_prefetch=1,
        grid=(batch * n_heads, nq, nk),
        in_specs=[
            pl.BlockSpec((2, tq, LANES), lambda b, i, j, s: (pair0 + b % n_heads, (b // n_heads) * nq + i, 0)),
            pl.BlockSpec((2, tk, LANES),
                         lambda b, i, j, s: (pair0 + n_heads + b % n_heads, (b // n_heads) * nk + kv_blk(i, j), 0)),
            pl.BlockSpec((2, tk, LANES),
                         lambda b, i, j, s: (pair0 + 2 * n_heads + b % n_heads, (b // n_heads) * nk + kv_blk(i, j), 0)),
            vec, vec, vec, vec,
            pl.BlockSpec((1, dv), lambda b, i, j, s: (0, 0)),
        ],
        out_specs=pl.BlockSpec((tq, dv), lambda b, i, j, s: ((b // n_heads) * nq + i, b % n_heads)),
        scratch_shapes=[pltpu.VMEM((2, tq, LANES), F32), pltpu.VMEM((2, tq, LANES), F32),
                        pltpu.VMEM((2, tq, dv), F32)],
    )
    return pl.pallas_call(
        functools.partial(_diff_kernel, tq=tq, tk=tk, n_heads=n_heads, scale=HEAD_DIM ** -0.5,
                          lam_init=lam_init),
        grid_spec=grid_spec,
        out_shape=jax.ShapeDtypeStruct((n, n_heads * dv), BF16),
        compiler_params=_params("parallel", "parallel", "arbitrary"),
        name="diff",
    )(slopes, proj, proj, proj, lq1, lk1, lq2, lk2, g)


def _outproj_kernel(x_ref, fox_ref, diff_ref, wa_ref, wb_ref, g_ref, wr_ref, br_ref,
                    h_ref, xn_ref, eid_ref, gate_ref, *, n_groups, per_group):
    h1 = x_ref[...] + (_dot(fox_ref[...], wa_ref[...]) + _dot(diff_ref[...], wb_ref[...]))
    h_ref[...] = h1
    xn = _rms(h1, g_ref[...], NORM_EPS)
    xn_ref[...] = xn
    lg = _dot_3pass(xn, wr_ref[...]) + br_ref[...]

    lane = lax.broadcasted_iota(jnp.int32, lg.shape, 1).astype(F32)
    big = float(LANES)

    def first_max(vals):
        m = jnp.max(vals, axis=1, keepdims=True)
        idx = jnp.min(jnp.where(vals == m, lane, big), axis=1, keepdims=True)
        return m, idx

    gl = jnp.where(lane < n_groups, lg, NEG_BIG)
    gmax, gidx = first_max(gl)
    g_prob = 1.0 / jnp.sum(jnp.where(lane < n_groups, jnp.exp(lg - gmax), 0.0), axis=1, keepdims=True)
    lo = n_groups + gidx * per_group
    el = jnp.where(jnp.logical_and(lane >= lo, lane < lo + per_group), lg, NEG_BIG)
    m1, i1 = first_max(el)
    m2, i2 = first_max(jnp.where(lane == i1, NEG_BIG, el))
    t = jnp.exp(m2 - m1)
    w1 = 1.0 / (1.0 + t)
    w2 = t / (1.0 + t)
    eid = jnp.where(lane == 0.0, i1 - n_groups, jnp.where(lane == 1.0, i2 - n_groups, 0.0))
    eid_ref[...] = eid.astype(jnp.int32)
    gate_ref[...] = jnp.where(lane == 0.0, g_prob * w1, jnp.where(lane == 1.0, g_prob * w2, 0.0))


def _outproj_router(xf, fox, diff, wa, wb, g, wr, br, n_groups, per_group, tm):
    n, d = xf.shape
    row = lambda w: pl.BlockSpec((tm, w), lambda i: (i, 0))
    full = lambda a: pl.BlockSpec(a.shape, lambda i: (0, 0))
    return pl.pallas_call(
        functools.partial(_outproj_kernel, n_groups=n_groups, per_group=per_group),
        grid=(n // tm,),
        in_specs=[row(d), row(fox.shape[1]), row(diff.shape[1]), full(wa), full(wb), full(g), full(wr), full(br)],
        out_specs=[row(d), row(d), row(LANES), row(LANES)],
        out_shape=[jax.ShapeDtypeStruct((n, d), F32), jax.ShapeDtypeStruct((n, d), F32),
                   jax.ShapeDtypeStruct((n, LANES), jnp.int32), jax.ShapeDtypeStruct((n, LANES), F32)],
        compiler_params=_params("parallel"),
        name="outproj",
    )(xf, fox, diff, wa, wb, g, wr, br)


def _onehots(eid_ref):
    eid = eid_ref[...]
    lane = lax.broadcasted_iota(jnp.int32, eid.shape, 1)
    return [lane == eid[:, k:k + 1] for k in range(TOP_K)]


def _rank_kernel(eid_ref, rank_ref, cnt_ref, carry):
    @pl.when(pl.program_id(0) == 0)
    def _():
        carry[...] = jnp.zeros_like(carry)

    ohs = _onehots(eid_ref)
    tm = rank_ref.shape[0]
    both = sum(oh.astype(F32) for oh in ohs)
    r = lax.broadcasted_iota(jnp.int32, (tm, tm), 0)
    c = lax.broadcasted_iota(jnp.int32, (tm, tm), 1)
    before = _dot((r > c).astype(BF16), both.astype(BF16)) + carry[...]
    lane = lax.broadcasted_iota(jnp.int32, rank_ref.shape, 1)
    out = jnp.zeros(rank_ref.shape, F32)
    for k, oh in enumerate(ohs):
        rk = jnp.sum(jnp.where(oh, before, 0.0), axis=1, keepdims=True)
        out = jnp.where(lane == k, rk, out)
    rank_ref[...] = out.astype(jnp.int32)
    carry[...] = carry[...] + jnp.sum(both, axis=0, keepdims=True)
    cnt_ref[...] = jnp.broadcast_to(carry[...], cnt_ref.shape).astype(jnp.int32)


def _expert_ranks(eid, tm):
    n = eid.shape[0]
    return pl.pallas_call(
        _rank_kernel,
        grid=(n // tm,),
        in_specs=[pl.BlockSpec((tm, LANES), lambda i: (i, 0))],
        out_specs=[pl.BlockSpec((tm, LANES), lambda i: (i, 0)),
                   pl.BlockSpec((SUBLANES, LANES), lambda i: (0, 0))],
        out_shape=[jax.ShapeDtypeStruct((n, LANES), jnp.int32),
                   jax.ShapeDtypeStruct((SUBLANES, LANES), jnp.int32)],
        scratch_shapes=[pltpu.VMEM((1, LANES), F32)],
        compiler_params=_params("arbitrary"),
        name="rank",
    )(eid)


def _dest_kernel(eid_ref, rank_ref, start_ref, dest_ref):
    start = start_ref[...].astype(F32)
    rank = rank_ref[...]
    lane = lax.broadcasted_iota(jnp.int32, dest_ref.shape, 1)
    out = jnp.zeros(dest_ref.shape, jnp.int32)
    for k, oh in enumerate(_onehots(eid_ref)):
        base = jnp.sum(jnp.where(oh, start, 0.0), axis=1, keepdims=True).astype(jnp.int32)
        out = jnp.where(lane == k, base + rank[:, k:k + 1], out)
    dest_ref[...] = out


def _dest_slots(eid, rank, start, tm):
    n = eid.shape[0]
    row = pl.BlockSpec((tm, LANES), lambda i: (i, 0))
    return pl.pallas_call(
        _dest_kernel,
        grid=(n // tm,),
        in_specs=[row, row, pl.BlockSpec((1, LANES), lambda i: (0, 0))],
        out_specs=row,
        out_shape=jax.ShapeDtypeStruct((n, LANES), jnp.int32),
        compiler_params=_params("parallel"),
        name="dest",
    )(eid, rank, start)


def _row_copy_wait(src, dst, sem, count):
    def body(t, c):
        pltpu.make_async_copy(src.at[pl.ds(0, 1)], dst.at[pl.ds(0, 1)], sem).wait()
        return c
    lax.fori_loop(0, count, body, 0)


def _dispatch_kernel(dest_ref, xn_hbm, zeros_hbm, rows_hbm, sem, *, tm):
    del zeros_hbm
    base = pl.program_id(0) * tm

    def body(t, c):
        for k in range(TOP_K):
            pltpu.make_async_copy(xn_hbm.at[pl.ds(base + t, 1)],
                                  rows_hbm.at[pl.ds(dest_ref[0, 0, TOP_K * t + k], 1)], sem).start()
        return c

    lax.fori_loop(0, tm, body, 0)
    _row_copy_wait(xn_hbm, rows_hbm, sem, TOP_K * tm)


def _dispatch(dest_blocks, xn, n_rows, tm):
    n, d = xn.shape
    return pl.pallas_call(
        functools.partial(_dispatch_kernel, tm=tm),
        grid=(n // tm,),
        in_specs=[pl.BlockSpec((1, 1, TOP_K * tm), lambda i: (i, 0, 0), memory_space=pltpu.SMEM),
                  pl.BlockSpec(memory_space=pl.ANY),
                  pl.BlockSpec(memory_space=pl.ANY)],
        out_specs=pl.BlockSpec(memory_space=pl.ANY),
        out_shape=jax.ShapeDtypeStruct((n_rows, d), F32),
        scratch_shapes=[pltpu.SemaphoreType.DMA],
        input_output_aliases={2: 0},
        compiler_params=_params("arbitrary"),
        name="dispatch",
    )(dest_blocks, xn, jnp.zeros((n_rows, d), F32))


def _moe_kernel(sb_e, sb_start, sb_n, n_sb, x_hbm, wg_ref, wu_ref, wd_ref, y_hbm,
                xin, xb, yacc, wg16, wu16, wd16, sem_in, sem_out, *, max_chunks):
    del sb_e, n_sb
    s, f = pl.program_id(0), pl.program_id(1)
    nf = pl.num_programs(1)
    n = sb_n[s]
    row0 = sb_start[s] * ROW_PAD

    def chunk_in(c):
        return pltpu.make_async_copy(x_hbm.at[pl.ds(row0 + c * ROW_PAD, ROW_PAD)], xin.at[c % 2], sem_in.at[c % 2])

    def chunk_out(c):
        return pltpu.make_async_copy(yacc.at[pl.ds(c * ROW_PAD, ROW_PAD)],
                                     y_hbm.at[pl.ds(row0 + c * ROW_PAD, ROW_PAD)], sem_out)

    @pl.when(jnp.logical_and(f == 0, n > 0))
    def _():
        chunk_in(0).start()
        for c in range(max_chunks):
            @pl.when(c < n)
            def _():
                if c + 1 < max_chunks:
                    @pl.when(c + 1 < n)
                    def _():
                        chunk_in(c + 1).start()
                chunk_in(c).wait()
                xb[pl.ds(c * ROW_PAD, ROW_PAD), :] = xin[c % 2].astype(BF16)

    @pl.when(n > 0)
    def _():
        wg16[...] = wg_ref[...].astype(BF16)
        wu16[...] = wu_ref[...].astype(BF16)
        wd16[...] = wd_ref[...].astype(BF16)

        def body(c, carry):
            r0 = pl.multiple_of(c * ROW_PAD, ROW_PAD)
            x = xb[pl.ds(r0, ROW_PAD), :]
            hid = jax.nn.silu(_dot(x, wg16[...])) * _dot(x, wu16[...])
            y = _dot(hid.astype(BF16), wd16[...])

            @pl.when(f == 0)
            def _():
                yacc[pl.ds(r0, ROW_PAD), :] = y

            @pl.when(f > 0)
            def _():
                yacc[pl.ds(r0, ROW_PAD), :] += y
            return carry

        lax.fori_loop(0, n, body, 0)

    @pl.when(jnp.logical_and(f == nf - 1, n > 0))
    def _():
        for c in range(max_chunks):
            @pl.when(c < n)
            def _():
                chunk_out(c).start()
        for c in range(max_chunks):
            @pl.when(c < n)
            def _():
                chunk_out(c).wait()


def _moe_experts(sched, x_rows, w_gate, w_up, w_down, max_super, max_chunks, ft):
    sb_e, sb_start, sb_n, n_sb = sched
    n_rows, d = x_rows.shape
    ff = w_gate.shape[-1]
    nf = ff // ft
    rmax = max_chunks * ROW_PAD

    def f_blk(s, f, n_sb):
        return jnp.where(s < n_sb[0], f, nf - 1)

    grid_spec = pltpu.PrefetchScalarGridSpec(
        num_scalar_prefetch=4,
        grid=(max_super, nf),
        in_specs=[
            pl.BlockSpec(memory_space=pl.ANY),
            pl.BlockSpec((None, d, ft), lambda s, f, e, st, nn, ns: (e[s], 0, f_blk(s, f, ns))),
            pl.BlockSpec((None, d, ft), lambda s, f, e, st, nn, ns: (e[s], 0, f_blk(s, f, ns))),
            pl.BlockSpec((None, ft, d), lambda s, f, e, st, nn, ns: (e[s], f_blk(s, f, ns), 0)),
        ],
        out_specs=pl.BlockSpec(memory_space=pl.ANY),
        scratch_shapes=[pltpu.VMEM((2, ROW_PAD, d), F32), pltpu.VMEM((rmax, d), BF16), pltpu.VMEM((rmax, d), F32),
                        pltpu.VMEM((d, ft), BF16), pltpu.VMEM((d, ft), BF16), pltpu.VMEM((ft, d), BF16),
                        pltpu.SemaphoreType.DMA((2,)), pltpu.SemaphoreType.DMA],
    )
    return pl.pallas_call(
        functools.partial(_moe_kernel, max_chunks=max_chunks),
        grid_spec=grid_spec,
        out_shape=jax.ShapeDtypeStruct((n_rows, d), F32),
        input_output_aliases={4: 0},
        compiler_params=_params("arbitrary", "arbitrary"),
        name="moe",
    )(sb_e, sb_start, sb_n, n_sb, x_rows, w_gate, w_up, w_down)


def _moe_schedule(counts, n_experts, max_super, max_chunks):
    chunks = (counts + ROW_PAD - 1) // ROW_PAD
    chunk_end = jnp.cumsum(chunks)
    chunk_start = chunk_end - chunks
    supers = (chunks + max_chunks - 1) // max_chunks
    super_end = jnp.cumsum(supers)
    n_sb = super_end[-1]
    sidx = jnp.arange(max_super, dtype=jnp.int32)
    e = jnp.minimum(jnp.searchsorted(super_end, sidx, side="right"), n_experts - 1).astype(jnp.int32)
    within = sidx - (super_end - supers)[e]
    start = chunk_start[e] + within * max_chunks
    nch = jnp.clip(chunks[e] - within * max_chunks, 0, max_chunks)
    live = sidx < n_sb
    last_e = e[jnp.maximum(n_sb - 1, 0)]
    return (jnp.where(live, e, last_e).astype(jnp.int32), jnp.where(live, start, 0).astype(jnp.int32),
            jnp.where(live, nch, 0).astype(jnp.int32), n_sb.reshape(1).astype(jnp.int32)), chunk_start * ROW_PAD


def _combine_kernel(dest_ref, h_ref, gate_ref, g_ref, y_hbm, o_ref, ybuf, sem, *, tm, final):
    def body(t, c):
        for k in range(TOP_K):
            pltpu.make_async_copy(y_hbm.at[pl.ds(dest_ref[0, 0, TOP_K * t + k], 1)],
                                  ybuf.at[k, pl.ds(t, 1)], sem).start()
        return c

    lax.fori_loop(0, tm, body, 0)
    _row_copy_wait(y_hbm, ybuf.at[0], sem, TOP_K * tm)
    gates = gate_ref[...]
    y = ybuf[0] * gates[:, 0:1] + ybuf[1] * gates[:, 1:2]
    h2 = h_ref[...] + y
    o_ref[...] = _rms(h2, g_ref[...], NORM_EPS) if final else h2


def _combine(dest_blocks, h1, gates, g, y_rows, tm, final):
    n, d = h1.shape
    return pl.pallas_call(
        functools.partial(_combine_kernel, tm=tm, final=final),
        grid=(n // tm,),
        in_specs=[pl.BlockSpec((1, 1, TOP_K * tm), lambda i: (i, 0, 0), memory_space=pltpu.SMEM),
                  pl.BlockSpec((tm, d), lambda i: (i, 0)),
                  pl.BlockSpec((tm, LANES), lambda i: (i, 0)),
                  pl.BlockSpec((1, d), lambda i: (0, 0)),
                  pl.BlockSpec(memory_space=pl.ANY)],
        out_specs=pl.BlockSpec((tm, d), lambda i: (i, 0)),
        out_shape=jax.ShapeDtypeStruct((n, d), F32),
        scratch_shapes=[pltpu.VMEM((TOP_K, tm, d), F32), pltpu.SemaphoreType.DMA],
        compiler_params=_params("arbitrary"),
        name="combine",
    )(dest_blocks, h1, gates, g, y_rows)


def _tile(full, want):
    t = min(full, want)
    assert full % t == 0, (full, want)
    return t


def kernel(x, attn_norm_g, w_in, forget_bias, lambda_q1, lambda_k1, lambda_q2, lambda_k2, diff_subln_g, w_out, ffn_norm_g, router_group_w, router_group_b, router_expert_w, router_expert_b, w_gate, w_up, w_down, final_norm_g):
    batch, seq, d = x.shape
    n = batch * seq
    depth = w_in.shape[0]
    fox_heads = forget_bias.shape[-1]
    fox_w = fox_heads * HEAD_DIM
    diff_w = (w_in.shape[-1] - 3 * fox_w - fox_heads) // 3
    diff_heads = diff_w // (2 * HEAD_DIM)
    n_groups = router_group_w.shape[-1]
    n_experts = router_expert_w.shape[-1]
    per_group = n_experts // n_groups
    d_ff = w_gate.shape[-1]
    assert fox_heads <= SUBLANES and n_groups + n_experts <= LANES and n_experts <= LANES
    assert fox_w % (2 * HEAD_DIM) == 0 and d % LANES == 0

    tm_proj, tn_proj = _tile(n, 512), _tile(3 * (fox_w + diff_w), 768)
    tm_forget = _tile(seq, 512)
    tq, tk = _tile(seq, 512), _tile(seq, 512)
    tm_out = _tile(n, 256)
    tm_rank = _tile(n, 1024)
    tm_rows = _tile(n, 256)
    max_chunks = MAX_CHUNKS
    ft = _tile(d_ff, 512)
    n_chunks_max = (n * TOP_K + n_experts * (ROW_PAD - 1)) // ROW_PAD
    max_super = n_experts + n_chunks_max // max_chunks
    n_rows = (n_chunks_max + max_chunks) * ROW_PAD

    slopes = jnp.asarray(2.0 ** (-8.0 * np.arange(1, diff_heads + 1) / diff_heads), dtype=F32)
    h = x.reshape(n, d)
    for l in range(depth):
        lam_init = 0.8 - 0.6 * math.exp(-0.3 * l)
        w = w_in[l]
        w_main = jnp.concatenate([w[:, :3 * fox_w], w[:, 3 * fox_w + fox_heads:]], axis=1).astype(BF16)
        w_forget = jnp.pad(w[:, 3 * fox_w:3 * fox_w + fox_heads], ((0, 0), (0, LANES - fox_heads)))
        proj, ff = _inproj(h, attn_norm_g[l][None, :], w_main, w_forget, tm_proj, tn_proj)
        fb = jnp.pad(forget_bias[l], (0, LANES - fox_heads))[None, :]
        ccol, crow = _forget_cumsum(ff, fb, seq, tm_forget, SUBLANES)
        fox = _fox_attention(proj, ccol, crow.reshape(SUBLANES, 1, n), batch, seq, fox_heads, tq, tk)
        vec = lambda a: a[l][None, :].astype(F32)
        diff = _diff_attention(proj, slopes, vec(lambda_q1), vec(lambda_k1), vec(lambda_q2), vec(lambda_k2),
                               vec(diff_subln_g), batch, seq, diff_heads, 3 * fox_w // (2 * HEAD_DIM),
                               lam_init, tq, tk)
        wo = w_out[l].astype(BF16)
        wr = jnp.pad(jnp.concatenate([router_group_w[l], router_expert_w[l]], axis=1),
                     ((0, 0), (0, LANES - n_groups - n_experts)))
        br = jnp.pad(jnp.concatenate([router_group_b[l], router_expert_b[l]]),
                     (0, LANES - n_groups - n_experts))[None, :]
        h1, xn, eid, gates = _outproj_router(h, fox, diff, wo[:fox_w], wo[fox_w:], ffn_norm_g[l][None, :],
                                             wr, br, n_groups, per_group, tm_out)
        rank, counts = _expert_ranks(eid, tm_rank)
        sched, row_start = _moe_schedule(counts[0, :n_experts], n_experts, max_super, max_chunks)
        start = jnp.pad(row_start.astype(jnp.int32), (0, LANES - n_experts))[None, :]
        dest = _dest_slots(eid, rank, start, tm_rank)
        dest_blocks = dest[:, :TOP_K].reshape(n // tm_rows, 1, TOP_K * tm_rows)
        x_rows = _dispatch(dest_blocks, xn, n_rows, tm_rows)
        y_rows = _moe_experts(sched, x_rows, w_gate[l], w_up[l], w_down[l], max_super, max_chunks, ft)
        h = _combine(dest_blocks, h1, gates, final_norm_g[None, :], y_rows, tm_rows, l == depth - 1)
    return h.reshape(batch, seq, d)
```

```python
import functools
import math

import jax
import jax.numpy as jnp
import numpy as np
from jax import lax
from jax.experimental import pallas as pl
from jax.experimental.pallas import tpu as pltpu

HEAD_DIM = 128
LANES = 128
SUBLANES = 8
NORM_EPS = 1e-6
SUBLN_EPS = 1e-5
TOP_K = 2
ROW_PAD = 128
MAX_CHUNKS = 8
ATTN_UNROLL = 4
NEG_BIG = -1e30
LOG2E = math.log2(math.e)
BIAS_PIECES = 3
VMEM_LIMIT_BYTES = 56 * 1024 * 1024

F32 = jnp.float32
BF16 = jnp.bfloat16
_dot = functools.partial(jnp.dot, preferred_element_type=F32)


def _params(*sem):
    return pltpu.CompilerParams(dimension_semantics=sem, vmem_limit_bytes=VMEM_LIMIT_BYTES)


def _split_bf16(a):
    hi = a.astype(BF16)
    lo = (a - hi.astype(F32)).astype(BF16)
    return hi, lo


def _dot_3pass(a, w):
    a_hi, a_lo = _split_bf16(a)
    w_hi, w_lo = _split_bf16(w)
    return _dot(a_hi, w_hi) + _dot(a_lo, w_hi) + _dot(a_hi, w_lo)


def _rms(x, g, eps):
    return x * lax.rsqrt(jnp.mean(x * x, axis=-1, keepdims=True) + eps) * g


def _three_pieces(u):
    h1 = u.astype(BF16).astype(F32)
    r1 = u - h1
    h2 = r1.astype(BF16).astype(F32)
    h3 = r1 - h2
    lane = lax.broadcasted_iota(jnp.int32, (u.shape[0], LANES), 1)
    return jnp.where(lane == 0, h1, jnp.where(lane == 1, h2, jnp.where(lane == 2, h3, 0.0))).astype(BF16)


def _inproj_kernel(x_ref, g_ref, w_ref, wf_ref, o_ref, ff_ref, a_scr):
    @pl.when(pl.program_id(1) == 0)
    def _():
        a = _rms(x_ref[...], g_ref[...], NORM_EPS)
        a_scr[...] = a.astype(BF16)
        ff_ref[...] = _dot_3pass(a, wf_ref[...])

    acc = _dot(a_scr[...], w_ref[...])
    for c in range(o_ref.shape[0]):
        o_ref[c] = acc[:, c * LANES:(c + 1) * LANES].astype(o_ref.dtype)


def _inproj(xf, g, w_main, w_forget, tm, tn):
    n, d = xf.shape
    cols = w_main.shape[1]
    assert tn % LANES == 0 and cols % tn == 0
    return pl.pallas_call(
        _inproj_kernel,
        grid=(n // tm, cols // tn),
        in_specs=[pl.BlockSpec((tm, d), lambda i, j: (i, 0)),
                  pl.BlockSpec((1, d), lambda i, j: (0, 0)),
                  pl.BlockSpec((d, tn), lambda i, j: (0, j)),
                  pl.BlockSpec((d, LANES), lambda i, j: (0, 0))],
        out_specs=[pl.BlockSpec((tn // LANES, tm, LANES), lambda i, j: (j, i, 0)),
                   pl.BlockSpec((tm, LANES), lambda i, j: (i, 0))],
        out_shape=[jax.ShapeDtypeStruct((cols // LANES, n, LANES), BF16),
                   jax.ShapeDtypeStruct((n, LANES), F32)],
        scratch_shapes=[pltpu.VMEM((tm, d), BF16)],
        compiler_params=_params("parallel", "arbitrary"),
        name="inproj",
    )(xf, g, w_main, w_forget)


def _forget_kernel(ff_ref, fb_ref, kaug_ref, carry, *, tiles_per_seq):
    @pl.when(pl.program_id(0) % tiles_per_seq == 0)
    def _():
        carry[...] = jnp.zeros_like(carry)

    lf = jax.nn.log_sigmoid(ff_ref[...] + fb_ref[...])
    tm = lf.shape[0]
    r = lax.broadcasted_iota(jnp.int32, (tm, tm), 0)
    c = lax.broadcasted_iota(jnp.int32, (tm, tm), 1)
    tri = (r >= c).astype(BF16)
    h1 = lf.astype(BF16)
    r1 = lf - h1.astype(F32)
    h2 = r1.astype(BF16)
    h3 = (r1 - h2.astype(F32)).astype(BF16)
    cs = (_dot(tri, h1) + _dot(tri, h2)) + _dot(tri, h3) + carry[...]
    carry[...] = cs[tm - 1:tm, :]
    for h in range(kaug_ref.shape[0]):
        kaug_ref[h] = _three_pieces(cs[:, h:h + 1] * (-LOG2E))


def _forget_cumsum(ff, fb, seq, tm, n_heads):
    n = ff.shape[0]
    return pl.pallas_call(
        functools.partial(_forget_kernel, tiles_per_seq=seq // tm),
        grid=(n // tm,),
        in_specs=[pl.BlockSpec((tm, LANES), lambda i: (i, 0)),
                  pl.BlockSpec((1, LANES), lambda i: (0, 0))],
        out_specs=pl.BlockSpec((n_heads, tm, LANES), lambda i: (0, i, 0)),
        out_shape=jax.ShapeDtypeStruct((n_heads, n, LANES), BF16),
        scratch_shapes=[pltpu.VMEM((1, LANES), F32)],
        compiler_params=_params("arbitrary"),
        name="forget",
    )(ff, fb)


def _lanes(x, width):
    reps = width // LANES
    return x if reps == 1 else jnp.concatenate([x] * reps, axis=1)


def _attn_tile(q_aug, kc, v, state, mask):
    m_prev, l_prev, acc = state
    s = lax.dot_general(q_aug, kc, (((1,), (1,)), ((), ())), preferred_element_type=F32)
    if mask is not None:
        s = jnp.where(mask, s, NEG_BIG)
    m_new = jnp.maximum(m_prev, jnp.max(s, axis=1, keepdims=True))
    alpha = jnp.exp2(m_prev - m_new)
    p = jnp.exp2(s - _lanes(m_new, s.shape[1]))
    l_new = alpha * l_prev + jnp.sum(p, axis=1, keepdims=True)
    acc_new = _lanes(alpha, acc.shape[1]) * acc + _dot(p.astype(BF16), v)
    return m_new, l_new, acc_new


def _causal_attention(q_ref, k_ref, kaug_ref, v_ref, tile, n_maps):
    i = pl.program_id(1)
    ones = (lax.broadcasted_iota(jnp.int32, (tile, LANES), 1) < BIAS_PIECES).astype(BF16)
    q_aug = [jnp.concatenate([q_ref[mp], ones], axis=1) for mp in range(n_maps)]
    dv = v_ref.shape[0] * LANES

    def step(j, state, mask):
        r0 = pl.multiple_of(j * tile, tile)
        ka = kaug_ref[pl.ds(r0, tile), :]
        v = _cat_lanes([v_ref[c, pl.ds(r0, tile), :] for c in range(v_ref.shape[0])])
        return tuple(
            _attn_tile(q_aug[mp], jnp.concatenate([k_ref[mp, pl.ds(r0, tile), :], ka], axis=1), v, state[mp], mask)
            for mp in range(n_maps))

    init = tuple((jnp.full((tile, LANES), NEG_BIG, F32), jnp.zeros((tile, LANES), F32), jnp.zeros((tile, dv), F32))
                 for _ in range(n_maps))

    def body(jj, state):
        for u in range(ATTN_UNROLL):
            state = step(jj * ATTN_UNROLL + u, state, None)
        return state

    state = lax.fori_loop(0, i // ATTN_UNROLL, body, init)
    state = lax.fori_loop((i // ATTN_UNROLL) * ATTN_UNROLL, i, lambda j, st: step(j, st, None), state)
    rows = lax.broadcasted_iota(jnp.int32, (tile, tile), 0)
    cols = lax.broadcasted_iota(jnp.int32, (tile, tile), 1)
    return step(i, state, rows >= cols)


def _cat_lanes(parts):
    return parts[0] if len(parts) == 1 else jnp.concatenate(parts, axis=1)


def _fox_kernel(q_ref, k_ref, kaug_ref, v_ref, o_ref, *, tile):
    (m, l, acc), = _causal_attention(q_ref, k_ref, kaug_ref, v_ref, tile, 1)
    o_ref[...] = (acc / l).astype(o_ref.dtype)


def _fox_attention(proj, kaug, batch, seq, n_heads, tile):
    n = batch * seq
    nq = seq // tile
    kv = lambda chunk0: pl.BlockSpec((1, seq, LANES), lambda b, i: (chunk0 + b % n_heads, b // n_heads, 0))
    return pl.pallas_call(
        functools.partial(_fox_kernel, tile=tile),
        grid=(batch * n_heads, nq),
        in_specs=[
            pl.BlockSpec((1, tile, LANES), lambda b, i: (b % n_heads, (b // n_heads) * nq + i, 0)),
            kv(n_heads),
            pl.BlockSpec((None, seq, LANES), lambda b, i: (b % n_heads, b // n_heads, 0)),
            kv(2 * n_heads),
        ],
        out_specs=pl.BlockSpec((tile, LANES), lambda b, i: ((b // n_heads) * nq + i, b % n_heads)),
        out_shape=jax.ShapeDtypeStruct((n, n_heads * HEAD_DIM), BF16),
        compiler_params=_params("parallel", "parallel"),
        name="fox",
    )(proj, proj, kaug, proj)


def _diff_kernel(q_ref, k_ref, kaug_ref, v_ref, lq1_ref, lk1_ref, lq2_ref, lk2_ref, g_ref, o_ref,
                 *, tile, lam_init):
    (m1, l1, acc1), (m2, l2, acc2) = _causal_attention(q_ref, k_ref, kaug_ref, v_ref, tile, 2)
    lam = (jnp.exp(jnp.sum(lq1_ref[...] * lk1_ref[...], axis=1, keepdims=True))
           - jnp.exp(jnp.sum(lq2_ref[...] * lk2_ref[...], axis=1, keepdims=True)) + lam_init)
    dv = acc1.shape[1]
    d = acc1 / _lanes(l1, dv) - lam * (acc2 / _lanes(l2, dv))
    o_ref[...] = (_rms(d, g_ref[...], SUBLN_EPS) * (1.0 - lam_init)).astype(o_ref.dtype)


def _diff_attention(proj, kaug, lq1, lk1, lq2, lk2, g, batch, seq, n_heads, pair0, lam_init, tile):
    n = batch * seq
    nq = seq // tile
    dv = 2 * HEAD_DIM
    kv = lambda p0: pl.BlockSpec((2, seq, LANES), lambda b, i: (p0 + b % n_heads, b // n_heads, 0))
    vec = pl.BlockSpec((1, HEAD_DIM), lambda b, i: (0, 0))
    return pl.pallas_call(
        functools.partial(_diff_kernel, tile=tile, lam_init=lam_init),
        grid=(batch * n_heads, nq),
        in_specs=[
            pl.BlockSpec((2, tile, LANES), lambda b, i: (pair0 + b % n_heads, (b // n_heads) * nq + i, 0)),
            kv(pair0 + n_heads),
            pl.BlockSpec((None, seq, LANES), lambda b, i: (b % n_heads, 0, 0)),
            kv(pair0 + 2 * n_heads),
            vec, vec, vec, vec,
            pl.BlockSpec((1, dv), lambda b, i: (0, 0)),
        ],
        out_specs=pl.BlockSpec((tile, dv), lambda b, i: ((b // n_heads) * nq + i, b % n_heads)),
        out_shape=jax.ShapeDtypeStruct((n, n_heads * dv), BF16),
        compiler_params=_params("parallel", "parallel"),
        name="diff",
    )(proj, proj, kaug, proj, lq1, lk1, lq2, lk2, g)


def _alibi_key_bias(n_heads, seq):
    slopes = 2.0 ** (-8.0 * np.arange(1, n_heads + 1) / n_heads)
    rem = jnp.asarray((slopes[:, None] * np.arange(seq)[None, :] * LOG2E).astype(np.float32))
    table = jnp.zeros((n_heads, seq, LANES), BF16)
    for idx in range(BIAS_PIECES):
        piece = rem.astype(BF16)
        table = table.at[:, :, idx].set(piece)
        rem = rem - piece.astype(F32)
    return table


def _outproj_kernel(x_ref, fox_ref, diff_ref, wa_ref, wb_ref, g_ref, wr_ref, br_ref,
                    h_ref, xn_ref, eid_ref, gate_ref, *, n_groups, per_group):
    h1 = x_ref[...] + (_dot(fox_ref[...], wa_ref[...]) + _dot(diff_ref[...], wb_ref[...]))
    h_ref[...] = h1
    xn = _rms(h1, g_ref[...], NORM_EPS)
    xn_ref[...] = xn
    lg = _dot_3pass(xn, wr_ref[...]) + br_ref[...]

    lane = lax.broadcasted_iota(jnp.int32, lg.shape, 1).astype(F32)
    big = float(LANES)

    def first_max(vals):
        m = jnp.max(vals, axis=1, keepdims=True)
        idx = jnp.min(jnp.where(vals == m, lane, big), axis=1, keepdims=True)
        return m, idx

    gl = jnp.where(lane < n_groups, lg, NEG_BIG)
    gmax, gidx = first_max(gl)
    g_prob = 1.0 / jnp.sum(jnp.where(lane < n_groups, jnp.exp(lg - gmax), 0.0), axis=1, keepdims=True)
    lo = n_groups + gidx * per_group
    el = jnp.where(jnp.logical_and(lane >= lo, lane < lo + per_group), lg, NEG_BIG)
    m1, i1 = first_max(el)
    m2, i2 = first_max(jnp.where(lane == i1, NEG_BIG, el))
    t = jnp.exp(m2 - m1)
    w1 = 1.0 / (1.0 + t)
    w2 = t / (1.0 + t)
    eid = jnp.where(lane == 0.0, i1 - n_groups, jnp.where(lane == 1.0, i2 - n_groups, 0.0))
    eid_ref[...] = eid.astype(jnp.int32)
    gate_ref[...] = jnp.where(lane == 0.0, g_prob * w1, jnp.where(lane == 1.0, g_prob * w2, 0.0))


def _outproj_router(xf, fox, diff, wa, wb, g, wr, br, n_groups, per_group, tm):
    n, d = xf.shape
    row = lambda w: pl.BlockSpec((tm, w), lambda i: (i, 0))
    full = lambda a: pl.BlockSpec(a.shape, lambda i: (0, 0))
    return pl.pallas_call(
        functools.partial(_outproj_kernel, n_groups=n_groups, per_group=per_group),
        grid=(n // tm,),
        in_specs=[row(d), row(fox.shape[1]), row(diff.shape[1]), full(wa), full(wb), full(g), full(wr), full(br)],
        out_specs=[row(d), row(d), row(LANES), row(LANES)],
        out_shape=[jax.ShapeDtypeStruct((n, d), F32), jax.ShapeDtypeStruct((n, d), F32),
                   jax.ShapeDtypeStruct((n, LANES), jnp.int32), jax.ShapeDtypeStruct((n, LANES), F32)],
        compiler_params=_params("parallel"),
        name="outproj",
    )(xf, fox, diff, wa, wb, g, wr, br)


def _onehots(eid_ref):
    eid = eid_ref[...]
    lane = lax.broadcasted_iota(jnp.int32, eid.shape, 1)
    return [lane == eid[:, k:k + 1] for k in range(TOP_K)]


def _rank_kernel(eid_ref, rank_ref, cnt_ref, carry):
    @pl.when(pl.program_id(0) == 0)
    def _():
        carry[...] = jnp.zeros_like(carry)

    ohs = _onehots(eid_ref)
    tm = rank_ref.shape[0]
    both = sum(oh.astype(F32) for oh in ohs)
    r = lax.broadcasted_iota(jnp.int32, (tm, tm), 0)
    c = lax.broadcasted_iota(jnp.int32, (tm, tm), 1)
    before = _dot((r > c).astype(BF16), both.astype(BF16)) + carry[...]
    lane = lax.broadcasted_iota(jnp.int32, rank_ref.shape, 1)
    out = jnp.zeros(rank_ref.shape, F32)
    for k, oh in enumerate(ohs):
        rk = jnp.sum(jnp.where(oh, before, 0.0), axis=1, keepdims=True)
        out = jnp.where(lane == k, rk, out)
    rank_ref[...] = out.astype(jnp.int32)
    carry[...] = carry[...] + jnp.sum(both, axis=0, keepdims=True)
    cnt_ref[...] = jnp.broadcast_to(carry[...], cnt_ref.shape).astype(jnp.int32)


def _expert_ranks(eid, tm):
    n = eid.shape[0]
    return pl.pallas_call(
        _rank_kernel,
        grid=(n // tm,),
        in_specs=[pl.BlockSpec((tm, LANES), lambda i: (i, 0))],
        out_specs=[pl.BlockSpec((tm, LANES), lambda i: (i, 0)),
                   pl.BlockSpec((SUBLANES, LANES), lambda i: (0, 0))],
        out_shape=[jax.ShapeDtypeStruct((n, LANES), jnp.int32),
                   jax.ShapeDtypeStruct((SUBLANES, LANES), jnp.int32)],
        scratch_shapes=[pltpu.VMEM((1, LANES), F32)],
        compiler_params=_params("arbitrary"),
        name="rank",
    )(eid)


def _dest_kernel(eid_ref, rank_ref, start_ref, dest_ref):
    start = start_ref[...].astype(F32)
    rank = rank_ref[...]
    lane = lax.broadcasted_iota(jnp.int32, dest_ref.shape, 1)
    out = jnp.zeros(dest_ref.shape, jnp.int32)
    for k, oh in enumerate(_onehots(eid_ref)):
        base = jnp.sum(jnp.where(oh, start, 0.0), axis=1, keepdims=True).astype(jnp.int32)
        out = jnp.where(lane == k, base + rank[:, k:k + 1], out)
    dest_ref[...] = out


def _dest_slots(eid, rank, start, tm):
    n = eid.shape[0]
    row = pl.BlockSpec((tm, LANES), lambda i: (i, 0))
    return pl.pallas_call(
        _dest_kernel,
        grid=(n // tm,),
        in_specs=[row, row, pl.BlockSpec((1, LANES), lambda i: (0, 0))],
        out_specs=row,
        out_shape=jax.ShapeDtypeStruct((n, LANES), jnp.int32),
        compiler_params=_params("parallel"),
        name="dest",
    )(eid, rank, start)


def _row_copy_wait(src, dst, sem, count):
    def body(t, c):
        pltpu.make_async_copy(src.at[pl.ds(0, 1)], dst.at[pl.ds(0, 1)], sem).wait()
        return c
    lax.fori_loop(0, count, body, 0)


def _dispatch_kernel(dest_ref, xn_ref, zeros_hbm, rows_hbm, sem, *, tm):
    del zeros_hbm

    def body(t, c):
        for k in range(TOP_K):
            pltpu.make_async_copy(xn_ref.at[pl.ds(t, 1)],
                                  rows_hbm.at[pl.ds(dest_ref[0, 0, TOP_K * t + k], 1)], sem).start()
        return c

    lax.fori_loop(0, tm, body, 0)
    _row_copy_wait(xn_ref, rows_hbm, sem, TOP_K * tm)


def _dispatch(dest_blocks, xn, n_rows, tm):
    n, d = xn.shape
    return pl.pallas_call(
        functools.partial(_dispatch_kernel, tm=tm),
        grid=(n // tm,),
        in_specs=[pl.BlockSpec((1, 1, TOP_K * tm), lambda i: (i, 0, 0), memory_space=pltpu.SMEM),
                  pl.BlockSpec((tm, d), lambda i: (i, 0)),
                  pl.BlockSpec(memory_space=pl.ANY)],
        out_specs=pl.BlockSpec(memory_space=pl.ANY),
        out_shape=jax.ShapeDtypeStruct((n_rows, d), F32),
        scratch_shapes=[pltpu.SemaphoreType.DMA],
        input_output_aliases={2: 0},
        compiler_params=_params("arbitrary"),
        name="dispatch",
    )(dest_blocks, xn, jnp.zeros((n_rows, d), F32))


def _moe_kernel(sb_e, sb_start, sb_n, n_sb, x_hbm, wg_ref, wu_ref, wd_ref, y_hbm,
                xin, xb, yacc, wg16, wu16, wd16, sem_in, sem_out, *, max_chunks):
    del sb_e, n_sb
    s, f = pl.program_id(0), pl.program_id(1)
    nf = pl.num_programs(1)
    n = sb_n[s]
    row0 = sb_start[s] * ROW_PAD

    def chunk_in(c):
        return pltpu.make_async_copy(x_hbm.at[pl.ds(row0 + c * ROW_PAD, ROW_PAD)], xin.at[c % 2], sem_in.at[c % 2])

    def chunk_out(c):
        return pltpu.make_async_copy(yacc.at[pl.ds(c * ROW_PAD, ROW_PAD)],
                                     y_hbm.at[pl.ds(row0 + c * ROW_PAD, ROW_PAD)], sem_out)

    @pl.when(jnp.logical_and(f == 0, n > 0))
    def _():
        chunk_in(0).start()
        for c in range(max_chunks):
            @pl.when(c < n)
            def _():
                if c + 1 < max_chunks:
                    @pl.when(c + 1 < n)
                    def _():
                        chunk_in(c + 1).start()
                chunk_in(c).wait()
                xb[pl.ds(c * ROW_PAD, ROW_PAD), :] = xin[c % 2].astype(BF16)

    @pl.when(n > 0)
    def _():
        wg16[...] = wg_ref[...].astype(BF16)
        wu16[...] = wu_ref[...].astype(BF16)
        wd16[...] = wd_ref[...].astype(BF16)

        def body(c, carry):
            r0 = pl.multiple_of(c * ROW_PAD, ROW_PAD)
            x = xb[pl.ds(r0, ROW_PAD), :]
            hid = jax.nn.silu(_dot(x, wg16[...])) * _dot(x, wu16[...])
            y = _dot(hid.astype(BF16), wd16[...])

            @pl.when(f == 0)
            def _():
                yacc[pl.ds(r0, ROW_PAD), :] = y

            @pl.when(f > 0)
            def _():
                yacc[pl.ds(r0, ROW_PAD), :] += y
            return carry

        lax.fori_loop(0, n, body, 0)

    @pl.when(jnp.logical_and(f == nf - 1, n > 0))
    def _():
        for c in range(max_chunks):
            @pl.when(c < n)
            def _():
                chunk_out(c).start()
        for c in range(max_chunks):
            @pl.when(c < n)
            def _():
                chunk_out(c).wait()


def _moe_experts(sched, x_rows, w_gate, w_up, w_down, max_super, max_chunks, ft):
    sb_e, sb_start, sb_n, n_sb = sched
    n_rows, d = x_rows.shape
    ff = w_gate.shape[-1]
    nf = ff // ft
    rmax = max_chunks * ROW_PAD

    def f_blk(s, f, n_sb):
        return jnp.where(s < n_sb[0], f, nf - 1)

    grid_spec = pltpu.PrefetchScalarGridSpec(
        num_scalar_prefetch=4,
        grid=(max_super, nf),
        in_specs=[
            pl.BlockSpec(memory_space=pl.ANY),
            pl.BlockSpec((None, d, ft), lambda s, f, e, st, nn, ns: (e[s], 0, f_blk(s, f, ns))),
            pl.BlockSpec((None, d, ft), lambda s, f, e, st, nn, ns: (e[s], 0, f_blk(s, f, ns))),
            pl.BlockSpec((None, ft, d), lambda s, f, e, st, nn, ns: (e[s], f_blk(s, f, ns), 0)),
        ],
        out_specs=pl.BlockSpec(memory_space=pl.ANY),
        scratch_shapes=[pltpu.VMEM((2, ROW_PAD, d), F32), pltpu.VMEM((rmax, d), BF16), pltpu.VMEM((rmax, d), F32),
                        pltpu.VMEM((d, ft), BF16), pltpu.VMEM((d, ft), BF16), pltpu.VMEM((ft, d), BF16),
                        pltpu.SemaphoreType.DMA((2,)), pltpu.SemaphoreType.DMA],
    )
    return pl.pallas_call(
        functools.partial(_moe_kernel, max_chunks=max_chunks),
        grid_spec=grid_spec,
        out_shape=jax.ShapeDtypeStruct((n_rows, d), F32),
        input_output_aliases={4: 0},
        compiler_params=_params("arbitrary", "arbitrary"),
        name="moe",
    )(sb_e, sb_start, sb_n, n_sb, x_rows, w_gate, w_up, w_down)


def _moe_schedule(counts, n_experts, max_super, max_chunks):
    chunks = (counts + ROW_PAD - 1) // ROW_PAD
    chunk_end = jnp.cumsum(chunks)
    chunk_start = chunk_end - chunks
    supers = (chunks + max_chunks - 1) // max_chunks
    super_end = jnp.cumsum(supers)
    n_sb = super_end[-1]
    sidx = jnp.arange(max_super, dtype=jnp.int32)
    e = jnp.minimum(jnp.searchsorted(super_end, sidx, side="right"), n_experts - 1).astype(jnp.int32)
    within = sidx - (super_end - supers)[e]
    start = chunk_start[e] + within * max_chunks
    nch = jnp.clip(chunks[e] - within * max_chunks, 0, max_chunks)
    live = sidx < n_sb
    last_e = e[jnp.maximum(n_sb - 1, 0)]
    return (jnp.where(live, e, last_e).astype(jnp.int32), jnp.where(live, start, 0).astype(jnp.int32),
            jnp.where(live, nch, 0).astype(jnp.int32), n_sb.reshape(1).astype(jnp.int32)), chunk_start * ROW_PAD


def _combine_kernel(dest_ref, h_ref, gate_ref, g_ref, y_hbm, o_ref, ybuf, sem, *, tm, final):
    def body(t, c):
        for k in range(TOP_K):
            pltpu.make_async_copy(y_hbm.at[pl.ds(dest_ref[0, 0, TOP_K * t + k], 1)],
                                  ybuf.at[k, pl.ds(t, 1)], sem).start()
        return c

    lax.fori_loop(0, tm, body, 0)
    _row_copy_wait(y_hbm, ybuf.at[0], sem, TOP_K * tm)
    gates = gate_ref[...]
    y = ybuf[0] * gates[:, 0:1] + ybuf[1] * gates[:, 1:2]
    h2 = h_ref[...] + y
    o_ref[...] = _rms(h2, g_ref[...], NORM_EPS) if final else h2


def _combine(dest_blocks, h1, gates, g, y_rows, tm, final):
    n, d = h1.shape
    return pl.pallas_call(
        functools.partial(_combine_kernel, tm=tm, final=final),
        grid=(n // tm,),
        in_specs=[pl.BlockSpec((1, 1, TOP_K * tm), lambda i: (i, 0, 0), memory_space=pltpu.SMEM),
                  pl.BlockSpec((tm, d), lambda i: (i, 0)),
                  pl.BlockSpec((tm, LANES), lambda i: (i, 0)),
                  pl.BlockSpec((1, d), lambda i: (0, 0)),
                  pl.BlockSpec(memory_space=pl.ANY)],
        out_specs=pl.BlockSpec((tm, d), lambda i: (i, 0)),
        out_shape=jax.ShapeDtypeStruct((n, d), F32),
        scratch_shapes=[pltpu.VMEM((TOP_K, tm, d), F32), pltpu.SemaphoreType.DMA],
        compiler_params=_params("arbitrary"),
        name="combine",
    )(dest_blocks, h1, gates, g, y_rows)


def _tile(full, want):
    t = min(full, want)
    assert full % t == 0, (full, want)
    return t


def kernel(x, attn_norm_g, w_in, forget_bias, lambda_q1, lambda_k1, lambda_q2, lambda_k2, diff_subln_g, w_out, ffn_norm_g, router_group_w, router_group_b, router_expert_w, router_expert_b, w_gate, w_up, w_down, final_norm_g):
    batch, seq, d = x.shape
    n = batch * seq
    depth = w_in.shape[0]
    fox_heads = forget_bias.shape[-1]
    fox_w = fox_heads * HEAD_DIM
    diff_w = (w_in.shape[-1] - 3 * fox_w - fox_heads) // 3
    diff_heads = diff_w // (2 * HEAD_DIM)
    n_groups = router_group_w.shape[-1]
    n_experts = router_expert_w.shape[-1]
    per_group = n_experts // n_groups
    d_ff = w_gate.shape[-1]
    assert fox_heads <= LANES and n_groups + n_experts <= LANES and n_experts <= LANES
    assert fox_w % (2 * HEAD_DIM) == 0 and d % LANES == 0

    tm_proj, tn_proj = _tile(n, 512), _tile(3 * (fox_w + diff_w), 768)
    tm_forget = _tile(seq, 512)
    t_attn = _tile(seq, 512)
    tm_out = _tile(n, 256)
    tm_rank = _tile(n, 1024)
    tm_rows = _tile(n, 256)
    max_chunks = MAX_CHUNKS
    ft = _tile(d_ff, 512)
    n_chunks_max = (n * TOP_K + n_experts * (ROW_PAD - 1)) // ROW_PAD
    max_super = n_experts + n_chunks_max // max_chunks
    n_rows = (n_chunks_max + max_chunks) * ROW_PAD

    alibi = _alibi_key_bias(diff_heads, seq)
    qscale = HEAD_DIM ** -0.5 * LOG2E
    col_scale = jnp.concatenate([jnp.full((fox_w,), qscale, F32), jnp.ones((2 * fox_w,), F32),
                                 jnp.full((diff_w,), qscale, F32), jnp.ones((2 * diff_w,), F32)])
    h = x.reshape(n, d)
    for l in range(depth):
        lam_init = 0.8 - 0.6 * math.exp(-0.3 * l)
        w = w_in[l]
        w_main = (jnp.concatenate([w[:, :3 * fox_w], w[:, 3 * fox_w + fox_heads:]], axis=1)
                  * col_scale[None, :]).astype(BF16)
        w_forget = jnp.pad(w[:, 3 * fox_w:3 * fox_w + fox_heads], ((0, 0), (0, LANES - fox_heads)))
        proj, ff = _inproj(h, attn_norm_g[l][None, :], w_main, w_forget, tm_proj, tn_proj)
        fb = jnp.pad(forget_bias[l], (0, LANES - fox_heads))[None, :]
        kaug = _forget_cumsum(ff, fb, seq, tm_forget, fox_heads)
        fox = _fox_attention(proj, kaug, batch, seq, fox_heads, t_attn)
        vec = lambda a: a[l][None, :].astype(F32)
        diff = _diff_attention(proj, alibi, vec(lambda_q1), vec(lambda_k1), vec(lambda_q2), vec(lambda_k2),
                               vec(diff_subln_g), batch, seq, diff_heads, 3 * fox_w // (2 * HEAD_DIM),
                               lam_init, t_attn)
        wo = w_out[l].astype(BF16)
        wr = jnp.pad(jnp.concatenate([router_group_w[l], router_expert_w[l]], axis=1),
                     ((0, 0), (0, LANES - n_groups - n_experts)))
        br = jnp.pad(jnp.concatenate([router_group_b[l], router_expert_b[l]]),
                     (0, LANES - n_groups - n_experts))[None, :]
        h1, xn, eid, gates = _outproj_router(h, fox, diff, wo[:fox_w], wo[fox_w:], ffn_norm_g[l][None, :],
                                             wr, br, n_groups, per_group, tm_out)
        rank, counts = _expert_ranks(eid, tm_rank)
        sched, row_start = _moe_schedule(counts[0, :n_experts], n_experts, max_super, max_chunks)
        start = jnp.pad(row_start.astype(jnp.int32), (0, LANES - n_experts))[None, :]
        dest = _dest_slots(eid, rank, start, tm_rank)
        dest_blocks = dest[:, :TOP_K].reshape(n // tm_rows, 1, TOP_K * tm_rows)
        x_rows = _dispatch(dest_blocks, xn, n_rows, tm_rows)
        y_rows = _moe_experts(sched, x_rows, w_gate[l], w_up[l], w_down[l], max_super, max_chunks, ft)
        h = _combine(dest_blocks, h1, gates, final_norm_g[None, :], y_rows, tm_rows, l == depth - 1)
    return h.reshape(batch, seq, d)
```

```python
import functools
import math

import jax
import jax.numpy as jnp
import numpy as np
from jax import lax
from jax.experimental import pallas as pl
from jax.experimental.pallas import tpu as pltpu

HEAD_DIM = 128
LANES = 128
SUBLANES = 8
NORM_EPS = 1e-6
SUBLN_EPS = 1e-5
TOP_K = 2
ROW_PAD = 128
MAX_CHUNKS = 8
ATTN_UNROLL = 4
ROW_UNROLL = 8
NEG_BIG = -1e30
LOG2E = math.log2(math.e)
BIAS_PIECES = 3
VMEM_LIMIT_BYTES = 56 * 1024 * 1024

F32 = jnp.float32
BF16 = jnp.bfloat16
_dot = functools.partial(jnp.dot, preferred_element_type=F32)


def _params(*sem):
    return pltpu.CompilerParams(dimension_semantics=sem, vmem_limit_bytes=VMEM_LIMIT_BYTES)


def _split_bf16(a):
    hi = a.astype(BF16)
    lo = (a - hi.astype(F32)).astype(BF16)
    return hi, lo


def _dot_3pass(a, w):
    a_hi, a_lo = _split_bf16(a)
    w_hi, w_lo = _split_bf16(w)
    return _dot(a_hi, w_hi) + _dot(a_lo, w_hi) + _dot(a_hi, w_lo)


def _rms(x, g, eps):
    return x * lax.rsqrt(jnp.mean(x * x, axis=-1, keepdims=True) + eps) * g


def _three_pieces(u):
    h1 = u.astype(BF16).astype(F32)
    r1 = u - h1
    h2 = r1.astype(BF16).astype(F32)
    h3 = r1 - h2
    lane = lax.broadcasted_iota(jnp.int32, (u.shape[0], LANES), 1)
    return jnp.where(lane == 0, h1, jnp.where(lane == 1, h2, jnp.where(lane == 2, h3, 0.0))).astype(BF16)


def _inproj_kernel(x_ref, g_ref, w_ref, wf_ref, o_ref, ff_ref, a_scr):
    @pl.when(pl.program_id(1) == 0)
    def _():
        a = _rms(x_ref[...], g_ref[...], NORM_EPS)
        a_scr[...] = a.astype(BF16)
        ff_ref[...] = _dot_3pass(a, wf_ref[...])

    acc = _dot(a_scr[...], w_ref[...])
    for c in range(o_ref.shape[0]):
        o_ref[c] = acc[:, c * LANES:(c + 1) * LANES].astype(o_ref.dtype)


def _inproj(xf, g, w_main, w_forget, tm, tn):
    n, d = xf.shape
    cols = w_main.shape[1]
    assert tn % LANES == 0 and cols % tn == 0
    return pl.pallas_call(
        _inproj_kernel,
        grid=(n // tm, cols // tn),
        in_specs=[pl.BlockSpec((tm, d), lambda i, j: (i, 0)),
                  pl.BlockSpec((1, d), lambda i, j: (0, 0)),
                  pl.BlockSpec((d, tn), lambda i, j: (0, j)),
                  pl.BlockSpec((d, LANES), lambda i, j: (0, 0))],
        out_specs=[pl.BlockSpec((tn // LANES, tm, LANES), lambda i, j: (j, i, 0)),
                   pl.BlockSpec((tm, LANES), lambda i, j: (i, 0))],
        out_shape=[jax.ShapeDtypeStruct((cols // LANES, n, LANES), BF16),
                   jax.ShapeDtypeStruct((n, LANES), F32)],
        scratch_shapes=[pltpu.VMEM((tm, d), BF16)],
        compiler_params=_params("parallel", "arbitrary"),
        name="inproj",
    )(xf, g, w_main, w_forget)


def _forget_kernel(ff_ref, fb_ref, kaug_ref, carry, *, tiles_per_seq):
    @pl.when(pl.program_id(0) % tiles_per_seq == 0)
    def _():
        carry[...] = jnp.zeros_like(carry)

    lf = jax.nn.log_sigmoid(ff_ref[...] + fb_ref[...])
    tm = lf.shape[0]
    r = lax.broadcasted_iota(jnp.int32, (tm, tm), 0)
    c = lax.broadcasted_iota(jnp.int32, (tm, tm), 1)
    tri = (r >= c).astype(BF16)
    h1 = lf.astype(BF16)
    r1 = lf - h1.astype(F32)
    h2 = r1.astype(BF16)
    h3 = (r1 - h2.astype(F32)).astype(BF16)
    cs = (_dot(tri, h1) + _dot(tri, h2)) + _dot(tri, h3) + carry[...]
    carry[...] = cs[tm - 1:tm, :]
    for h in range(kaug_ref.shape[0]):
        kaug_ref[h] = _three_pieces(cs[:, h:h + 1] * (-LOG2E))


def _forget_cumsum(ff, fb, seq, tm, n_heads):
    n = ff.shape[0]
    return pl.pallas_call(
        functools.partial(_forget_kernel, tiles_per_seq=seq // tm),
        grid=(n // tm,),
        in_specs=[pl.BlockSpec((tm, LANES), lambda i: (i, 0)),
                  pl.BlockSpec((1, LANES), lambda i: (0, 0))],
        out_specs=pl.BlockSpec((n_heads, tm, LANES), lambda i: (0, i, 0)),
        out_shape=jax.ShapeDtypeStruct((n_heads, n, LANES), BF16),
        scratch_shapes=[pltpu.VMEM((1, LANES), F32)],
        compiler_params=_params("arbitrary"),
        name="forget",
    )(ff, fb)


def _lanes(x, width):
    reps = width // LANES
    return x if reps == 1 else jnp.concatenate([x] * reps, axis=1)


def _attn_tile(q_aug, kc, v, state, mask):
    m_prev, l_prev, acc = state
    s = lax.dot_general(q_aug, kc, (((1,), (1,)), ((), ())), preferred_element_type=F32)
    if mask is not None:
        s = jnp.where(mask, s, NEG_BIG)
    m_new = jnp.maximum(m_prev, jnp.max(s, axis=1, keepdims=True))
    alpha = jnp.exp2(m_prev - m_new)
    p = jnp.exp2(s - _lanes(m_new, s.shape[1]))
    l_new = alpha * l_prev + jnp.sum(p, axis=1, keepdims=True)
    acc_new = _lanes(alpha, acc.shape[1]) * acc + _dot(p.astype(BF16), v)
    return m_new, l_new, acc_new


def _causal_attention(q_ref, k_ref, kaug_ref, v_ref, tile, n_maps):
    i = pl.program_id(1)
    ones = (lax.broadcasted_iota(jnp.int32, (tile, LANES), 1) < BIAS_PIECES).astype(BF16)
    q_aug = [jnp.concatenate([q_ref[mp], ones], axis=1) for mp in range(n_maps)]
    dv = v_ref.shape[0] * LANES

    def step(j, state, mask):
        r0 = pl.multiple_of(j * tile, tile)
        ka = kaug_ref[pl.ds(r0, tile), :]
        v = _cat_lanes([v_ref[c, pl.ds(r0, tile), :] for c in range(v_ref.shape[0])])
        return tuple(
            _attn_tile(q_aug[mp], jnp.concatenate([k_ref[mp, pl.ds(r0, tile), :], ka], axis=1), v, state[mp], mask)
            for mp in range(n_maps))

    init = tuple((jnp.full((tile, LANES), NEG_BIG, F32), jnp.zeros((tile, LANES), F32), jnp.zeros((tile, dv), F32))
                 for _ in range(n_maps))

    def body(jj, state):
        for u in range(ATTN_UNROLL):
            state = step(jj * ATTN_UNROLL + u, state, None)
        return state

    state = lax.fori_loop(0, i // ATTN_UNROLL, body, init)
    state = lax.fori_loop((i // ATTN_UNROLL) * ATTN_UNROLL, i, lambda j, st: step(j, st, None), state)
    rows = lax.broadcasted_iota(jnp.int32, (tile, tile), 0)
    cols = lax.broadcasted_iota(jnp.int32, (tile, tile), 1)
    return step(i, state, rows >= cols)


def _cat_lanes(parts):
    return parts[0] if len(parts) == 1 else jnp.concatenate(parts, axis=1)


def _fox_kernel(q_ref, k_ref, kaug_ref, v_ref, o_ref, *, tile):
    (m, l, acc), = _causal_attention(q_ref, k_ref, kaug_ref, v_ref, tile, 1)
    o_ref[...] = (acc / l).astype(o_ref.dtype)


def _fox_attention(proj, kaug, batch, seq, n_heads, tile):
    n = batch * seq
    nq = seq // tile
    kv = lambda chunk0: pl.BlockSpec((1, seq, LANES), lambda b, i: (chunk0 + b % n_heads, b // n_heads, 0))
    return pl.pallas_call(
        functools.partial(_fox_kernel, tile=tile),
        grid=(batch * n_heads, nq),
        in_specs=[
            pl.BlockSpec((1, tile, LANES), lambda b, i: (b % n_heads, (b // n_heads) * nq + i, 0)),
            kv(n_heads),
            pl.BlockSpec((None, seq, LANES), lambda b, i: (b % n_heads, b // n_heads, 0)),
            kv(2 * n_heads),
        ],
        out_specs=pl.BlockSpec((tile, LANES), lambda b, i: ((b // n_heads) * nq + i, b % n_heads)),
        out_shape=jax.ShapeDtypeStruct((n, n_heads * HEAD_DIM), BF16),
        compiler_params=_params("parallel", "parallel"),
        name="fox",
    )(proj, proj, kaug, proj)


def _diff_kernel(q_ref, k_ref, kaug_ref, v_ref, lq1_ref, lk1_ref, lq2_ref, lk2_ref, g_ref, o_ref,
                 *, tile, lam_init):
    (m1, l1, acc1), (m2, l2, acc2) = _causal_attention(q_ref, k_ref, kaug_ref, v_ref, tile, 2)
    lam = (jnp.exp(jnp.sum(lq1_ref[...] * lk1_ref[...], axis=1, keepdims=True))
           - jnp.exp(jnp.sum(lq2_ref[...] * lk2_ref[...], axis=1, keepdims=True)) + lam_init)
    dv = acc1.shape[1]
    d = acc1 / _lanes(l1, dv) - lam * (acc2 / _lanes(l2, dv))
    o_ref[...] = (_rms(d, g_ref[...], SUBLN_EPS) * (1.0 - lam_init)).astype(o_ref.dtype)


def _diff_attention(proj, kaug, lq1, lk1, lq2, lk2, g, batch, seq, n_heads, pair0, lam_init, tile):
    n = batch * seq
    nq = seq // tile
    dv = 2 * HEAD_DIM
    kv = lambda p0: pl.BlockSpec((2, seq, LANES), lambda b, i: (p0 + b % n_heads, b // n_heads, 0))
    vec = pl.BlockSpec((1, HEAD_DIM), lambda b, i: (0, 0))
    return pl.pallas_call(
        functools.partial(_diff_kernel, tile=tile, lam_init=lam_init),
        grid=(batch * n_heads, nq),
        in_specs=[
            pl.BlockSpec((2, tile, LANES), lambda b, i: (pair0 + b % n_heads, (b // n_heads) * nq + i, 0)),
            kv(pair0 + n_heads),
            pl.BlockSpec((None, seq, LANES), lambda b, i: (b % n_heads, 0, 0)),
            kv(pair0 + 2 * n_heads),
            vec, vec, vec, vec,
            pl.BlockSpec((1, dv), lambda b, i: (0, 0)),
        ],
        out_specs=pl.BlockSpec((tile, dv), lambda b, i: ((b // n_heads) * nq + i, b % n_heads)),
        out_shape=jax.ShapeDtypeStruct((n, n_heads * dv), BF16),
        compiler_params=_params("parallel", "parallel"),
        name="diff",
    )(proj, proj, kaug, proj, lq1, lk1, lq2, lk2, g)


def _alibi_key_bias(n_heads, seq):
    slopes = 2.0 ** (-8.0 * np.arange(1, n_heads + 1) / n_heads)
    rem = jnp.asarray((slopes[:, None] * np.arange(seq)[None, :] * LOG2E).astype(np.float32))
    table = jnp.zeros((n_heads, seq, LANES), BF16)
    for idx in range(BIAS_PIECES):
        piece = rem.astype(BF16)
        table = table.at[:, :, idx].set(piece)
        rem = rem - piece.astype(F32)
    return table


def _outproj_kernel(x_ref, fox_ref, diff_ref, wa_ref, wb_ref, g_ref, wr_ref, br_ref,
                    h_ref, xn_ref, eid_ref, gate_ref, *, n_groups, per_group):
    h1 = x_ref[...] + (_dot(fox_ref[...], wa_ref[...]) + _dot(diff_ref[...], wb_ref[...]))
    h_ref[...] = h1
    xn = _rms(h1, g_ref[...], NORM_EPS)
    xn_ref[...] = xn
    lg = _dot_3pass(xn, wr_ref[...]) + br_ref[...]

    lane = lax.broadcasted_iota(jnp.int32, lg.shape, 1).astype(F32)
    big = float(LANES)

    def first_max(vals):
        m = jnp.max(vals, axis=1, keepdims=True)
        idx = jnp.min(jnp.where(vals == m, lane, big), axis=1, keepdims=True)
        return m, idx

    gl = jnp.where(lane < n_groups, lg, NEG_BIG)
    gmax, gidx = first_max(gl)
    g_prob = 1.0 / jnp.sum(jnp.where(lane < n_groups, jnp.exp(lg - gmax), 0.0), axis=1, keepdims=True)
    lo = n_groups + gidx * per_group
    el = jnp.where(jnp.logical_and(lane >= lo, lane < lo + per_group), lg, NEG_BIG)
    m1, i1 = first_max(el)
    m2, i2 = first_max(jnp.where(lane == i1, NEG_BIG, el))
    t = jnp.exp(m2 - m1)
    w1 = 1.0 / (1.0 + t)
    w2 = t / (1.0 + t)
    eid = jnp.where(lane == 0.0, i1 - n_groups, jnp.where(lane == 1.0, i2 - n_groups, 0.0))
    eid_ref[...] = eid.astype(jnp.int32)
    gate_ref[...] = jnp.where(lane == 0.0, g_prob * w1, jnp.where(lane == 1.0, g_prob * w2, 0.0))


def _outproj_router(xf, fox, diff, wa, wb, g, wr, br, n_groups, per_group, tm):
    n, d = xf.shape
    row = lambda w: pl.BlockSpec((tm, w), lambda i: (i, 0))
    full = lambda a: pl.BlockSpec(a.shape, lambda i: (0, 0))
    return pl.pallas_call(
        functools.partial(_outproj_kernel, n_groups=n_groups, per_group=per_group),
        grid=(n // tm,),
        in_specs=[row(d), row(fox.shape[1]), row(diff.shape[1]), full(wa), full(wb), full(g), full(wr), full(br)],
        out_specs=[row(d), row(d), row(LANES), row(LANES)],
        out_shape=[jax.ShapeDtypeStruct((n, d), F32), jax.ShapeDtypeStruct((n, d), F32),
                   jax.ShapeDtypeStruct((n, LANES), jnp.int32), jax.ShapeDtypeStruct((n, LANES), F32)],
        compiler_params=_params("parallel"),
        name="outproj",
    )(xf, fox, diff, wa, wb, g, wr, br)


def _onehots(eid_ref):
    eid = eid_ref[...]
    lane = lax.broadcasted_iota(jnp.int32, eid.shape, 1)
    return [lane == eid[:, k:k + 1] for k in range(TOP_K)]


def _rank_kernel(eid_ref, rank_ref, cnt_ref, carry):
    @pl.when(pl.program_id(0) == 0)
    def _():
        carry[...] = jnp.zeros_like(carry)

    ohs = _onehots(eid_ref)
    tm = rank_ref.shape[0]
    both = sum(oh.astype(F32) for oh in ohs)
    r = lax.broadcasted_iota(jnp.int32, (tm, tm), 0)
    c = lax.broadcasted_iota(jnp.int32, (tm, tm), 1)
    before = _dot((r > c).astype(BF16), both.astype(BF16)) + carry[...]
    lane = lax.broadcasted_iota(jnp.int32, rank_ref.shape, 1)
    out = jnp.zeros(rank_ref.shape, F32)
    for k, oh in enumerate(ohs):
        rk = jnp.sum(jnp.where(oh, before, 0.0), axis=1, keepdims=True)
        out = jnp.where(lane == k, rk, out)
    rank_ref[...] = out.astype(jnp.int32)
    carry[...] = carry[...] + jnp.sum(both, axis=0, keepdims=True)
    cnt_ref[...] = jnp.broadcast_to(carry[...], cnt_ref.shape).astype(jnp.int32)


def _expert_ranks(eid, tm):
    n = eid.shape[0]
    return pl.pallas_call(
        _rank_kernel,
        grid=(n // tm,),
        in_specs=[pl.BlockSpec((tm, LANES), lambda i: (i, 0))],
        out_specs=[pl.BlockSpec((tm, LANES), lambda i: (i, 0)),
                   pl.BlockSpec((SUBLANES, LANES), lambda i: (0, 0))],
        out_shape=[jax.ShapeDtypeStruct((n, LANES), jnp.int32),
                   jax.ShapeDtypeStruct((SUBLANES, LANES), jnp.int32)],
        scratch_shapes=[pltpu.VMEM((1, LANES), F32)],
        compiler_params=_params("arbitrary"),
        name="rank",
    )(eid)


def _dest_kernel(eid_ref, rank_ref, start_ref, dest_ref):
    start = start_ref[...].astype(F32)
    rank = rank_ref[...]
    lane = lax.broadcasted_iota(jnp.int32, dest_ref.shape, 1)
    out = jnp.zeros(dest_ref.shape, jnp.int32)
    for k, oh in enumerate(_onehots(eid_ref)):
        base = jnp.sum(jnp.where(oh, start, 0.0), axis=1, keepdims=True).astype(jnp.int32)
        out = jnp.where(lane == k, base + rank[:, k:k + 1], out)
    dest_ref[...] = out


def _dest_slots(eid, rank, start, tm):
    n = eid.shape[0]
    row = pl.BlockSpec((tm, LANES), lambda i: (i, 0))
    return pl.pallas_call(
        _dest_kernel,
        grid=(n // tm,),
        in_specs=[row, row, pl.BlockSpec((1, LANES), lambda i: (0, 0))],
        out_specs=row,
        out_shape=jax.ShapeDtypeStruct((n, LANES), jnp.int32),
        compiler_params=_params("parallel"),
        name="dest",
    )(eid, rank, start)


def _for_rows(tm, fn):
    def body(g, c):
        for u in range(ROW_UNROLL):
            fn(g * ROW_UNROLL + u)
        return c
    lax.fori_loop(0, tm // ROW_UNROLL, body, 0)


def _dispatch_kernel(dest_ref, xn_ref, zeros_hbm, rows_hbm, sem, *, tm):
    del zeros_hbm

    def send(t):
        for k in range(TOP_K):
            pltpu.make_async_copy(xn_ref.at[pl.ds(t, 1)],
                                  rows_hbm.at[pl.ds(dest_ref[0, 0, TOP_K * t + k], 1)], sem).start()

    _for_rows(tm, send)
    for k in range(TOP_K):
        pltpu.make_async_copy(xn_ref, rows_hbm.at[pl.ds(0, tm)], sem).wait()


def _dispatch(dest_blocks, xn, n_rows, tm):
    n, d = xn.shape
    return pl.pallas_call(
        functools.partial(_dispatch_kernel, tm=tm),
        grid=(n // tm,),
        in_specs=[pl.BlockSpec((1, 1, TOP_K * tm), lambda i: (i, 0, 0), memory_space=pltpu.SMEM),
                  pl.BlockSpec((tm, d), lambda i: (i, 0)),
                  pl.BlockSpec(memory_space=pl.ANY)],
        out_specs=pl.BlockSpec(memory_space=pl.ANY),
        out_shape=jax.ShapeDtypeStruct((n_rows, d), F32),
        scratch_shapes=[pltpu.SemaphoreType.DMA],
        input_output_aliases={2: 0},
        compiler_params=_params("arbitrary"),
        name="dispatch",
    )(dest_blocks, xn, jnp.zeros((n_rows, d), F32))


def _moe_kernel(sb_e, sb_start, sb_n, n_sb, x_hbm, wg_ref, wu_ref, wd_ref, y_hbm,
                xin, yacc, sem_in, sem_out, *, max_chunks):
    del sb_e, n_sb
    s, f = pl.program_id(0), pl.program_id(1)
    ns, nf = pl.num_programs(0), pl.num_programs(1)
    n = sb_n[s]
    slot = s % 2

    def x_copy(step, c):
        row0 = (sb_start[step] + c) * ROW_PAD
        return pltpu.make_async_copy(x_hbm.at[pl.ds(row0, ROW_PAD)],
                                     xin.at[step % 2, pl.ds(c * ROW_PAD, ROW_PAD)], sem_in.at[step % 2])

    def y_copy(step, c):
        row0 = (sb_start[step] + c) * ROW_PAD
        return pltpu.make_async_copy(yacc.at[pl.ds(c * ROW_PAD, ROW_PAD)], y_hbm.at[pl.ds(row0, ROW_PAD)], sem_out)

    def for_chunks(step, fn):
        for c in range(max_chunks):
            @pl.when(c < sb_n[step])
            def _():
                fn(c)

    @pl.when(f == 0)
    def _():
        @pl.when(s == 0)
        def _():
            for_chunks(0, lambda c: x_copy(0, c).start())

        @pl.when(s + 1 < ns)
        def _():
            for_chunks(s + 1, lambda c: x_copy(s + 1, c).start())

        @pl.when(s > 0)
        def _():
            for_chunks(s - 1, lambda c: y_copy(s - 1, c).wait())

        for_chunks(s, lambda c: x_copy(s, c).wait())

    def compute(r0, rows):
        x = xin[slot, pl.ds(r0, rows), :].astype(BF16)
        hid = jax.nn.silu(_dot(x, wg_ref[...].astype(BF16))) * _dot(x, wu_ref[...].astype(BF16))
        y = _dot(hid.astype(BF16), wd_ref[...].astype(BF16))

        @pl.when(f == 0)
        def _():
            yacc[pl.ds(r0, rows), :] = y

        @pl.when(f > 0)
        def _():
            yacc[pl.ds(r0, rows), :] += y

    def pair_body(p, carry):
        compute(pl.multiple_of(p * (2 * ROW_PAD), 2 * ROW_PAD), 2 * ROW_PAD)
        return carry

    lax.fori_loop(0, n // 2, pair_body, 0)

    @pl.when(n % 2 == 1)
    def _():
        compute(pl.multiple_of((n - 1) * ROW_PAD, ROW_PAD), ROW_PAD)

    @pl.when(f == nf - 1)
    def _():
        for_chunks(s, lambda c: y_copy(s, c).start())

        @pl.when(s == ns - 1)
        def _():
            for_chunks(s, lambda c: y_copy(s, c).wait())


def _moe_experts(sched, x_rows, w_gate, w_up, w_down, max_super, max_chunks, ft):
    sb_e, sb_start, sb_n, n_sb = sched
    n_rows, d = x_rows.shape
    ff = w_gate.shape[-1]
    nf = ff // ft
    rmax = max_chunks * ROW_PAD

    def f_blk(s, f, n_sb):
        return jnp.where(s < n_sb[0], f, nf - 1)

    grid_spec = pltpu.PrefetchScalarGridSpec(
        num_scalar_prefetch=4,
        grid=(max_super, nf),
        in_specs=[
            pl.BlockSpec(memory_space=pl.ANY),
            pl.BlockSpec((None, d, ft), lambda s, f, e, st, nn, ns: (e[s], 0, f_blk(s, f, ns))),
            pl.BlockSpec((None, d, ft), lambda s, f, e, st, nn, ns: (e[s], 0, f_blk(s, f, ns))),
            pl.BlockSpec((None, ft, d), lambda s, f, e, st, nn, ns: (e[s], f_blk(s, f, ns), 0)),
        ],
        out_specs=pl.BlockSpec(memory_space=pl.ANY),
        scratch_shapes=[pltpu.VMEM((2, rmax, d), F32), pltpu.VMEM((rmax, d), F32),
                        pltpu.SemaphoreType.DMA((2,)), pltpu.SemaphoreType.DMA],
    )
    return pl.pallas_call(
        functools.partial(_moe_kernel, max_chunks=max_chunks),
        grid_spec=grid_spec,
        out_shape=jax.ShapeDtypeStruct((n_rows, d), F32),
        input_output_aliases={4: 0},
        compiler_params=_params("arbitrary", "arbitrary"),
        name="moe",
    )(sb_e, sb_start, sb_n, n_sb, x_rows, w_gate, w_up, w_down)


def _moe_schedule(counts, n_experts, max_super, max_chunks):
    chunks = (counts + ROW_PAD - 1) // ROW_PAD
    chunk_end = jnp.cumsum(chunks)
    chunk_start = chunk_end - chunks
    supers = (chunks + max_chunks - 1) // max_chunks
    super_end = jnp.cumsum(supers)
    n_sb = super_end[-1]
    sidx = jnp.arange(max_super, dtype=jnp.int32)
    e = jnp.minimum(jnp.searchsorted(super_end, sidx, side="right"), n_experts - 1).astype(jnp.int32)
    within = sidx - (super_end - supers)[e]
    start = chunk_start[e] + within * max_chunks
    nch = jnp.clip(chunks[e] - within * max_chunks, 0, max_chunks)
    live = sidx < n_sb
    last_e = e[jnp.maximum(n_sb - 1, 0)]
    return (jnp.where(live, e, last_e).astype(jnp.int32), jnp.where(live, start, 0).astype(jnp.int32),
            jnp.where(live, nch, 0).astype(jnp.int32), n_sb.reshape(1).astype(jnp.int32)), chunk_start * ROW_PAD


def _combine_kernel(dest_ref, h_ref, gate_ref, g_ref, y_hbm, o_ref, ybuf, sem, *, tm, final):
    def fetch(t):
        for k in range(TOP_K):
            pltpu.make_async_copy(y_hbm.at[pl.ds(dest_ref[0, 0, TOP_K * t + k], 1)],
                                  ybuf.at[k, pl.ds(t, 1)], sem).start()

    _for_rows(tm, fetch)
    for k in range(TOP_K):
        pltpu.make_async_copy(y_hbm.at[pl.ds(0, tm)], ybuf.at[k], sem).wait()
    gates = gate_ref[...]
    y = ybuf[0] * gates[:, 0:1] + ybuf[1] * gates[:, 1:2]
    h2 = h_ref[...] + y
    o_ref[...] = _rms(h2, g_ref[...], NORM_EPS) if final else h2


def _combine(dest_blocks, h1, gates, g, y_rows, tm, final):
    n, d = h1.shape
    return pl.pallas_call(
        functools.partial(_combine_kernel, tm=tm, final=final),
        grid=(n // tm,),
        in_specs=[pl.BlockSpec((1, 1, TOP_K * tm), lambda i: (i, 0, 0), memory_space=pltpu.SMEM),
                  pl.BlockSpec((tm, d), lambda i: (i, 0)),
                  pl.BlockSpec((tm, LANES), lambda i: (i, 0)),
                  pl.BlockSpec((1, d), lambda i: (0, 0)),
                  pl.BlockSpec(memory_space=pl.ANY)],
        out_specs=pl.BlockSpec((tm, d), lambda i: (i, 0)),
        out_shape=jax.ShapeDtypeStruct((n, d), F32),
        scratch_shapes=[pltpu.VMEM((TOP_K, tm, d), F32), pltpu.SemaphoreType.DMA],
        compiler_params=_params("arbitrary"),
        name="combine",
    )(dest_blocks, h1, gates, g, y_rows)


def _tile(full, want):
    t = min(full, want)
    assert full % t == 0, (full, want)
    return t


def kernel(x, attn_norm_g, w_in, forget_bias, lambda_q1, lambda_k1, lambda_q2, lambda_k2, diff_subln_g, w_out, ffn_norm_g, router_group_w, router_group_b, router_expert_w, router_expert_b, w_gate, w_up, w_down, final_norm_g):
    batch, seq, d = x.shape
    n = batch * seq
    depth = w_in.shape[0]
    fox_heads = forget_bias.shape[-1]
    fox_w = fox_heads * HEAD_DIM
    diff_w = (w_in.shape[-1] - 3 * fox_w - fox_heads) // 3
    diff_heads = diff_w // (2 * HEAD_DIM)
    n_groups = router_group_w.shape[-1]
    n_experts = router_expert_w.shape[-1]
    per_group = n_experts // n_groups
    d_ff = w_gate.shape[-1]
    assert fox_heads <= LANES and n_groups + n_experts <= LANES and n_experts <= LANES
    assert fox_w % (2 * HEAD_DIM) == 0 and d % LANES == 0

    tm_proj, tn_proj = _tile(n, 512), _tile(3 * (fox_w + diff_w), 768)
    tm_forget = _tile(seq, 512)
    t_attn = _tile(seq, 512)
    tm_out = _tile(n, 256)
    tm_rank = _tile(n, 1024)
    tm_rows = _tile(n, 256)
    max_chunks = MAX_CHUNKS
    ft = _tile(d_ff, 512)
    n_chunks_max = (n * TOP_K + n_experts * (ROW_PAD - 1)) // ROW_PAD
    max_super = n_experts + n_chunks_max // max_chunks
    n_rows = (n_chunks_max + max_chunks) * ROW_PAD

    alibi = _alibi_key_bias(diff_heads, seq)
    qscale = HEAD_DIM ** -0.5 * LOG2E
    col_scale = jnp.concatenate([jnp.full((fox_w,), qscale, F32), jnp.ones((2 * fox_w,), F32),
                                 jnp.full((diff_w,), qscale, F32), jnp.ones((2 * diff_w,), F32)])
    h = x.reshape(n, d)
    for l in range(depth):
        lam_init = 0.8 - 0.6 * math.exp(-0.3 * l)
        w = w_in[l]
        w_main = (jnp.concatenate([w[:, :3 * fox_w], w[:, 3 * fox_w + fox_heads:]], axis=1)
                  * col_scale[None, :]).astype(BF16)
        w_forget = jnp.pad(w[:, 3 * fox_w:3 * fox_w + fox_heads], ((0, 0), (0, LANES - fox_heads)))
        proj, ff = _inproj(h, attn_norm_g[l][None, :], w_main, w_forget, tm_proj, tn_proj)
        fb = jnp.pad(forget_bias[l], (0, LANES - fox_heads))[None, :]
        kaug = _forget_cumsum(ff, fb, seq, tm_forget, fox_heads)
        fox = _fox_attention(proj, kaug, batch, seq, fox_heads, t_attn)
        vec = lambda a: a[l][None, :].astype(F32)
        diff = _diff_attention(proj, alibi, vec(lambda_q1), vec(lambda_k1), vec(lambda_q2), vec(lambda_k2),
                               vec(diff_subln_g), batch, seq, diff_heads, 3 * fox_w // (2 * HEAD_DIM),
                               lam_init, t_attn)
        wo = w_out[l].astype(BF16)
        wr = jnp.pad(jnp.concatenate([router_group_w[l], router_expert_w[l]], axis=1),
                     ((0, 0), (0, LANES - n_groups - n_experts)))
        br = jnp.pad(jnp.concatenate([router_group_b[l], router_expert_b[l]]),
                     (0, LANES - n_groups - n_experts))[None, :]
        h1, xn, eid, gates = _outproj_router(h, fox, diff, wo[:fox_w], wo[fox_w:], ffn_norm_g[l][None, :],
                                             wr, br, n_groups, per_group, tm_out)
        rank, counts = _expert_ranks(eid, tm_rank)
        sched, row_start = _moe_schedule(counts[0, :n_experts], n_experts, max_super, max_chunks)
        start = jnp.pad(row_start.astype(jnp.int32), (0, LANES - n_experts))[None, :]
        dest = _dest_slots(eid, rank, start, tm_rank)
        dest_blocks = dest[:, :TOP_K].reshape(n // tm_rows, 1, TOP_K * tm_rows)
        x_rows = _dispatch(dest_blocks, xn, n_rows, tm_rows)
        y_rows = _moe_experts(sched, x_rows, w_gate[l], w_up[l], w_down[l], max_super, max_chunks, ft)
        h = _combine(dest_blocks, h1, gates, final_norm_g[None, :], y_rows, tm_rows, l == depth - 1)
    return h.reshape(batch, seq, d)
```
